```python
import math
import jax, jax.numpy as jnp
from jax import lax
import numpy as np

D_MODEL = 2048
BATCH = 4
SEQ = 2048
DEPTH = 4

CHUNK = 64
D_MIX = D_MODEL
D_S5 = D_MIX // 2
D_CONV = D_MIX - D_S5
S5_P = 16
S5_G = D_S5 // S5_P
S5_N = 64
CONV_HEADS = 8
CONV_W = 3
D_IN = D_S5 + 3 * D_CONV
D_FF = 5632
LN_EPS = 1e-5
RMS_EPS = 1e-6
DEEPNORM_ALPHA = (2.0 * DEPTH) ** 0.25
DEEPNORM_BETA = (8.0 * DEPTH) ** -0.25

kernel_name = "hybrid_s5_shortconv_macaron_deepnorm"


def layer_norm(x, g, b):
    xf = x.astype(jnp.float32)
    mu = jnp.mean(xf, axis=-1, keepdims=True)
    xc = xf - mu
    var = jnp.mean(xc * xc, axis=-1, keepdims=True)
    y = xc * lax.rsqrt(var + LN_EPS)
    return (y * g.astype(jnp.float32) + b.astype(jnp.float32)).astype(x.dtype)


def rms_norm(x, g):
    xf = x.astype(jnp.float32)
    y = xf * lax.rsqrt(jnp.mean(xf * xf, axis=-1, keepdims=True) + RMS_EPS)
    return (y * g.astype(jnp.float32)).astype(x.dtype)


def swiglu(x, w_gate, w_up, w_down):
    return (jax.nn.silu(x @ w_gate) * (x @ w_up)) @ w_down


def s5_mixer(u, lam_re, lam_im, log_dt, b_re, b_im, c_re, c_im, d):
    bsz, seq, _ = u.shape
    f32 = jnp.float32
    uf = u.astype(f32).reshape(bsz, seq, S5_G, S5_P)
    lre = lam_re.astype(f32)
    lim = lam_im.astype(f32)
    dt = jnp.exp(log_dt.astype(f32))[:, None]
    mag = jnp.exp(lre * dt)
    ang = lim * dt
    ab_re = mag * jnp.cos(ang)
    ab_im = mag * jnp.sin(ang)
    den = lre * lre + lim * lim
    nr = ab_re - 1.0
    ni = ab_im
    q_re = (nr * lre + ni * lim) / den
    q_im = (ni * lre - nr * lim) / den
    br = b_re.astype(f32)
    bi = b_im.astype(f32)
    bb_re = q_re[..., None] * br - q_im[..., None] * bi
    bb_im = q_re[..., None] * bi + q_im[..., None] * br
    bu_re = jnp.einsum('blgp,gnp->blgn', uf, bb_re)
    bu_im = jnp.einsum('blgp,gnp->blgn', uf, bb_im)
    a_re = jnp.broadcast_to(ab_re, bu_re.shape)
    a_im = jnp.broadcast_to(ab_im, bu_im.shape)

    def combine(e1, e2):
        a1r, a1i, b1r, b1i = e1
        a2r, a2i, b2r, b2i = e2
        return (a1r * a2r - a1i * a2i,
                a1r * a2i + a1i * a2r,
                a2r * b1r - a2i * b1i + b2r,
                a2r * b1i + a2i * b1r + b2i)

    _, _, s_re, s_im = lax.associative_scan(combine, (a_re, a_im, bu_re, bu_im), axis=1)
    y = (jnp.einsum('blgn,gpn->blgp', s_re, c_re.astype(f32))
         - jnp.einsum('blgn,gpn->blgp', s_im, c_im.astype(f32))
         + d.astype(f32) * uf)
    return y.reshape(bsz, seq, S5_G * S5_P).astype(u.dtype)


def causal_depthwise_conv(z, w, b):
    seq = z.shape[1]
    zp = jnp.pad(z, ((0, 0), (CONV_W - 1, 0), (0, 0)))
    out = b
    for k in range(CONV_W):
        out = out + w[k] * zp[:, k:k + seq]
    return out


def hybrid_mixer(x, w_in, lam_re, lam_im, log_dt, b_re, b_im, c_re, c_im, d,
                 w_glu, conv_w, conv_b, g_s5, g_conv, w_out):
    proj = x @ w_in
    u = proj[..., :D_S5]
    gate_b = proj[..., D_S5:D_S5 + D_CONV]
    gate_c = proj[..., D_S5 + D_CONV:D_S5 + 2 * D_CONV]
    h = proj[..., D_S5 + 2 * D_CONV:]
    y = jax.nn.gelu(s5_mixer(u, lam_re, lam_im, log_dt, b_re, b_im, c_re, c_im, d))
    y = y * jax.nn.sigmoid(y @ w_glu)
    y = rms_norm(y, g_s5)
    z = gate_b * causal_depthwise_conv(gate_c * h, conv_w, conv_b)
    z = rms_norm(z, g_conv)
    return jnp.concatenate([y, z], axis=-1) @ w_out


def setup_inputs(seed: int = 0) -> dict:
    key = jax.random.key(seed)
    ks = jax.random.split(key, 32)
    f32 = jnp.float32

    def nrm(k, shape, scale):
        return jax.random.normal(k, shape, f32) * scale

    def gain(k, shape):
        return 1.0 + 0.02 * jax.random.normal(k, shape, f32)

    L = DEPTH
    n_idx = jnp.arange(S5_N, dtype=f32)
    inp = {}
    inp["x"] = jax.random.normal(ks[0], (BATCH, SEQ, D_MODEL), f32)
    inp["ffn1_gate"] = nrm(ks[1], (L, D_MODEL, D_FF), D_MODEL ** -0.5)
    inp["ffn1_up"] = nrm(ks[2], (L, D_MODEL, D_FF), D_MODEL ** -0.5)
    inp["ffn1_down"] = nrm(ks[3], (L, D_FF, D_MODEL), D_FF ** -0.5 * DEEPNORM_BETA)
    inp["ln1_g"] = gain(ks[4], (L, D_MODEL))
    inp["ln1_b"] = nrm(ks[5], (L, D_MODEL), 0.02)
    inp["w_in"] = nrm(ks[6], (L, D_MODEL, D_IN), D_MODEL ** -0.5)
    inp["s5_lam_re"] = -0.5 * jnp.exp(0.05 * jax.random.normal(ks[7], (L, S5_G, S5_N), f32))
    inp["s5_lam_im"] = math.pi * n_idx + 0.01 * jax.random.normal(ks[8], (L, S5_G, S5_N), f32)
    inp["s5_log_dt"] = jax.random.uniform(ks[9], (L, S5_G), f32, math.log(1e-3), math.log(1e-1))
    inp["s5_b_re"] = nrm(ks[10], (L, S5_G, S5_N, S5_P), (2.0 * S5_P) ** -0.5)
    inp["s5_b_im"] = nrm(ks[11], (L, S5_G, S5_N, S5_P), (2.0 * S5_P) ** -0.5)
    inp["s5_c_re"] = nrm(ks[12], (L, S5_G, S5_P, S5_N), (2.0 * S5_N) ** -0.5)
    inp["s5_c_im"] = nrm(ks[13], (L, S5_G, S5_P, S5_N), (2.0 * S5_N) ** -0.5)
    inp["s5_d"] = nrm(ks[14], (L, S5_G, S5_P), 1.0)
    inp["s5_w_glu"] = nrm(ks[15], (L, D_S5, D_S5), D_S5 ** -0.5)
    inp["conv_w"] = nrm(ks[16], (L, CONV_W, D_CONV), CONV_W ** -0.5)
    inp["conv_b"] = nrm(ks[17], (L, D_CONV), 0.02)
    inp["g_s5"] = gain(ks[18], (L, D_S5))
    inp["g_conv"] = gain(ks[19], (L, D_CONV))
    inp["w_out"] = nrm(ks[20], (L, D_MIX, D_MODEL), D_MIX ** -0.5 * DEEPNORM_BETA)
    inp["ln2_g"] = gain(ks[21], (L, D_MODEL))
    inp["ln2_b"] = nrm(ks[22], (L, D_MODEL), 0.02)
    inp["ffn2_gate"] = nrm(ks[23], (L, D_MODEL, D_FF), D_MODEL ** -0.5)
    inp["ffn2_up"] = nrm(ks[24], (L, D_MODEL, D_FF), D_MODEL ** -0.5)
    inp["ffn2_down"] = nrm(ks[25], (L, D_FF, D_MODEL), D_FF ** -0.5 * DEEPNORM_BETA)
    inp["ln3_g"] = gain(ks[26], (L, D_MODEL))
    inp["ln3_b"] = nrm(ks[27], (L, D_MODEL), 0.02)
    return inp


def reference(x, ffn1_gate, ffn1_up, ffn1_down, ln1_g, ln1_b, w_in, s5_lam_re, s5_lam_im,
              s5_log_dt, s5_b_re, s5_b_im, s5_c_re, s5_c_im, s5_d, s5_w_glu, conv_w, conv_b,
              g_s5, g_conv, w_out, ln2_g, ln2_b, ffn2_gate, ffn2_up, ffn2_down, ln3_g, ln3_b):
    for l in range(DEPTH):
        x = layer_norm(DEEPNORM_ALPHA * x + 0.5 * swiglu(x, ffn1_gate[l], ffn1_up[l], ffn1_down[l]),
                       ln1_g[l], ln1_b[l])
        m = hybrid_mixer(x, w_in[l], s5_lam_re[l], s5_lam_im[l], s5_log_dt[l], s5_b_re[l],
                         s5_b_im[l], s5_c_re[l], s5_c_im[l], s5_d[l], s5_w_glu[l], conv_w[l],
                         conv_b[l], g_s5[l], g_conv[l], w_out[l])
        x = layer_norm(DEEPNORM_ALPHA * x + m, ln2_g[l], ln2_b[l])
        x = layer_norm(DEEPNORM_ALPHA * x + 0.5 * swiglu(x, ffn2_gate[l], ffn2_up[l], ffn2_down[l]),
                       ln3_g[l], ln3_b[l])
    return x
```

```python
import functools

import jax
import jax.numpy as jnp
from jax import lax
from jax.experimental import pallas as pl
from jax.experimental.pallas import tpu as pltpu

F32 = jnp.float32
BF16 = jnp.bfloat16

LN_EPS = 1e-5
RMS_EPS = 1e-6
CONV_W = 3

LANES = 128
SUBLANES = 8
VMEM_LIMIT_BYTES = 56 * 1024 * 1024

S5_P = 16
S5_N = 64
S5_T = 8
S5_GPT = LANES // S5_P
S5_HALF = S5_GPT * S5_N
S5_STATE = 2 * S5_HALF
S5_ROW = S5_T * LANES


def _layer_norm(y, g, b):
    mu = jnp.mean(y, axis=-1, keepdims=True)
    yc = y - mu
    var = jnp.mean(yc * yc, axis=-1, keepdims=True)
    return yc * lax.rsqrt(var + LN_EPS) * g + b


def _rms_norm(y, g):
    return y * lax.rsqrt(jnp.mean(y * y, axis=-1, keepdims=True) + RMS_EPS) * g


def _sigmoid(x):
    return 1.0 / (1.0 + jnp.exp(-x))


def _gelu_tanh(x):
    c = 0.7978845608028654
    return 0.5 * x * (1.0 + jnp.tanh(c * (x + 0.044715 * (x * x * x))))


def _cmul(ar, ai, br, bi):
    return ar * br - ai * bi, ar * bi + ai * br


def _params(*sem):
    return pltpu.CompilerParams(dimension_semantics=sem, vmem_limit_bytes=VMEM_LIMIT_BYTES)


def _ffn_kernel(x_ref, wg_ref, wu_ref, wd_ref, g_ref, b_ref, o_ref, xb_ref, acc_ref, *, alpha):
    f = pl.program_id(1)

    @pl.when(f == 0)
    def _():
        xb_ref[...] = x_ref[...].astype(BF16)
        acc_ref[...] = jnp.zeros_like(acc_ref)

    xb = xb_ref[...]
    gate = jnp.dot(xb, wg_ref[...], preferred_element_type=F32)
    up = jnp.dot(xb, wu_ref[...], preferred_element_type=F32)
    h = (gate * _sigmoid(gate)) * up
    acc_ref[...] += jnp.dot(h.astype(BF16), wd_ref[...], preferred_element_type=F32)

    @pl.when(f == pl.num_programs(1) - 1)
    def _():
        y = alpha * x_ref[...] + 0.5 * acc_ref[...]
        o_ref[...] = _layer_norm(y, g_ref[...], b_ref[...])


def _ffn(x, wg, wu, wd, ln_g, ln_b, layer, alpha, tm=512, tf=512):
    m, d = x.shape
    ff = wg.shape[-1]
    return pl.pallas_call(
        functools.partial(_ffn_kernel, alpha=alpha),
        grid=(m // tm, ff // tf),
        in_specs=[
            pl.BlockSpec((tm, d), lambda i, f: (i, 0)),
            pl.BlockSpec((None, d, tf), lambda i, f: (layer, 0, f)),
            pl.BlockSpec((None, d, tf), lambda i, f: (layer, 0, f)),
            pl.BlockSpec((None, tf, d), lambda i, f: (layer, f, 0)),
            pl.BlockSpec((None, 1, d), lambda i, f: (layer, 0, 0)),
            pl.BlockSpec((None, 1, d), lambda i, f: (layer, 0, 0)),
        ],
        out_specs=pl.BlockSpec((tm, d), lambda i, f: (i, 0)),
        out_shape=jax.ShapeDtypeStruct((m, d), F32),
        scratch_shapes=[pltpu.VMEM((tm, d), BF16), pltpu.VMEM((tm, d), F32)],
        compiler_params=_params("parallel", "arbitrary"),
        name="ffn",
    )(x, wg, wu, wd, ln_g, ln_b)


def _proj_kernel(x_ref, win_ref, cw_ref, cb_ref, gc_ref, u_ref, z_ref, vpad_ref, *, tiles_per_seq):
    i = pl.program_id(0)
    tm = x_ref.shape[0]
    d_s5 = u_ref.shape[0] * LANES
    d_conv = z_ref.shape[1]
    xb = x_ref[...].astype(BF16)

    u = jnp.dot(xb, win_ref[:, 0:d_s5], preferred_element_type=F32)
    for j in range(u_ref.shape[0]):
        u_ref[j] = u[:, j * LANES:(j + 1) * LANES].astype(BF16)

    o = d_s5
    gate_c = jnp.dot(xb, win_ref[:, o + d_conv:o + 2 * d_conv], preferred_element_type=F32)
    hid = jnp.dot(xb, win_ref[:, o + 2 * d_conv:o + 3 * d_conv], preferred_element_type=F32)
    v = gate_c * hid

    @pl.when(i % tiles_per_seq == 0)
    def _():
        vpad_ref[0:SUBLANES, :] = jnp.zeros((SUBLANES, d_conv), F32)

    vpad_ref[SUBLANES:SUBLANES + tm, :] = v
    v1 = vpad_ref[SUBLANES - 1:SUBLANES - 1 + tm, :]
    v2 = vpad_ref[SUBLANES - 2:SUBLANES - 2 + tm, :]
    conv = cb_ref[...] + cw_ref[0:1, :] * v2 + cw_ref[1:2, :] * v1 + cw_ref[2:3, :] * v
    vpad_ref[0:SUBLANES, :] = vpad_ref[tm:tm + SUBLANES, :]

    gate_b = jnp.dot(xb, win_ref[:, o:o + d_conv], preferred_element_type=F32)
    z_ref[...] = _rms_norm(gate_b * conv, gc_ref[...]).astype(BF16)


def _proj(x, w_in, conv_w, conv_b, g_conv, layer, seq, d_s5, tm=256):
    m, d = x.shape
    d_in = w_in.shape[-1]
    d_conv = conv_b.shape[-1]
    assert CONV_W - 1 <= SUBLANES and conv_w.shape[1] == CONV_W and seq % tm == 0
    return pl.pallas_call(
        functools.partial(_proj_kernel, tiles_per_seq=seq // tm),
        grid=(m // tm,),
        in_specs=[
            pl.BlockSpec((tm, d), lambda i: (i, 0)),
            pl.BlockSpec((None, d, d_in), lambda i: (layer, 0, 0)),
            pl.BlockSpec((None, CONV_W, d_conv), lambda i: (layer, 0, 0)),
            pl.BlockSpec((None, 1, d_conv), lambda i: (layer, 0, 0)),
            pl.BlockSpec((None, 1, d_conv), lambda i: (layer, 0, 0)),
        ],
        out_specs=[
            pl.BlockSpec((d_s5 // LANES, tm, LANES), lambda i: (0, i, 0)),
            pl.BlockSpec((tm, d_conv), lambda i: (i, 0)),
        ],
        out_shape=[
            jax.ShapeDtypeStruct((d_s5 // LANES, m, LANES), BF16),
            jax.ShapeDtypeStruct((m, d_conv), BF16),
        ],
        scratch_shapes=[pltpu.VMEM((tm + SUBLANES, d_conv), F32)],
        compiler_params=_params("arbitrary"),
        name="proj_conv",
    )(x, w_in, conv_w, conv_b, g_conv)


def _discretize(lre, lim, ldt):
    dt = jnp.exp(ldt)
    mag = jnp.exp(lre * dt)
    ang = lim * dt
    return mag * jnp.cos(ang), mag * jnp.sin(ang)


def _s5_prep_kernel(bre_ref, bim_ref, cre_ref, cim_ref, lre_ref, lim_ref, ldt_ref,
                    slre_ref, slim_ref, sldt_ref, wt_ref, wl_ref, ws_ref, at_ref):
    lre, lim = lre_ref[...], lim_ref[...]
    ab_re, ab_im = _discretize(lre, lim, ldt_ref[...])
    den = lre * lre + lim * lim
    nr, ni = ab_re - 1.0, ab_im
    q_re = (nr * lre + ni * lim) / den
    q_im = (ni * lre - nr * lim) / den
    b_re, b_im = bre_ref[...], bim_ref[...]
    bb_re = q_re * b_re - q_im * b_im
    bb_im = q_re * b_im + q_im * b_re
    c_re, c_im = cre_ref[...], cim_ref[...]

    rows = lax.broadcasted_iota(jnp.int32, (S5_HALF, LANES), 0) // S5_N
    cols = lax.broadcasted_iota(jnp.int32, (S5_HALF, LANES), 1) // S5_P
    same_group = rows == cols

    def blk(a):
        return jnp.where(same_group, jnp.concatenate([a] * S5_GPT, axis=0), 0.0)

    cm_re, cm_im = blk(c_re), blk(c_im)
    p_re, p_im = jnp.ones_like(ab_re), jnp.zeros_like(ab_re)
    toeplitz = []
    for k in range(S5_T + 1):
        if k < S5_T:
            bl_re, bl_im = _cmul(p_re, p_im, bb_re, bb_im)
            blt_re, blt_im = blk(bl_re).T, blk(bl_im).T
            tau = S5_T - 1 - k
            wl_ref[tau * LANES:(tau + 1) * LANES, 0:S5_HALF] = blt_re.astype(BF16)
            wl_ref[tau * LANES:(tau + 1) * LANES, S5_HALF:S5_STATE] = blt_im.astype(BF16)
            kk = (jnp.dot(blt_re, cm_re, precision=lax.Precision.HIGHEST, preferred_element_type=F32)
                  - jnp.dot(blt_im, cm_im, precision=lax.Precision.HIGHEST, preferred_element_type=F32))
            toeplitz.append(kk.astype(BF16))
        if k >= 1:
            cl_re, cl_im = _cmul(p_re, p_im, c_re, c_im)
            t = k - 1
            ws_ref[0:S5_HALF, t * LANES:(t + 1) * LANES] = blk(cl_re).astype(BF16)
            ws_ref[S5_HALF:S5_STATE, t * LANES:(t + 1) * LANES] = (-blk(cl_im)).astype(BF16)
        p_re, p_im = _cmul(p_re, p_im, ab_re, ab_im)

    zero = jnp.zeros((LANES, LANES), BF16)
    for tau in range(S5_T):
        for t in range(S5_T):
            wt_ref[tau * LANES:(tau + 1) * LANES, t * LANES:(t + 1) * LANES] = (
                toeplitz[t - tau] if t >= tau else zero)

    a_re, a_im = _discretize(slre_ref[...], slim_ref[...], sldt_ref[...])
    step = 1
    while step < S5_T:
        a_re, a_im = _cmul(a_re, a_im, a_re, a_im)
        step *= 2
    at_ref[0:1, :] = a_re
    at_ref[1:2, :] = a_im


def _s5_prep(lam_re, lam_im, log_dt, b_re, b_im, c_re, c_im):
    depth, g, n = lam_re.shape
    p = b_re.shape[-1]
    assert (p, n) == (S5_P, S5_N) and S5_T & (S5_T - 1) == 0
    nt = g // S5_GPT

    def b_layout(a):
        return a.reshape(depth, nt, S5_GPT, n, p).transpose(0, 1, 3, 2, 4).reshape(depth, nt, n, LANES)

    def c_layout(a):
        return a.reshape(depth, nt, S5_GPT, p, n).transpose(0, 1, 4, 2, 3).reshape(depth, nt, n, LANES)

    def lam_layout(a):
        a = a.reshape(depth, nt, S5_GPT, n).transpose(0, 1, 3, 2)
        return jnp.broadcast_to(a[..., None], (depth, nt, n, S5_GPT, p)).reshape(depth, nt, n, LANES)

    ldt = jnp.broadcast_to(log_dt.reshape(depth, nt, 1, S5_GPT, 1), (depth, nt, n, S5_GPT, p))
    ldt = ldt.reshape(depth, nt, n, LANES)
    s_lre = lam_re.reshape(depth, nt, 1, S5_HALF)
    s_lim = lam_im.reshape(depth, nt, 1, S5_HALF)
    s_ldt = jnp.broadcast_to(log_dt.reshape(depth, nt, S5_GPT, 1), (depth, nt, S5_GPT, n))
    s_ldt = s_ldt.reshape(depth, nt, 1, S5_HALF)

    tile_spec = pl.BlockSpec((None, None, n, LANES), lambda l, j: (l, j, 0, 0))
    lane_spec = pl.BlockSpec((None, None, 1, S5_HALF), lambda l, j: (l, j, 0, 0))
    return pl.pallas_call(
        _s5_prep_kernel,
        grid=(depth, nt),
        in_specs=[tile_spec] * 7 + [lane_spec] * 3,
        out_specs=[
            pl.BlockSpec((None, None, S5_ROW, S5_ROW), lambda l, j: (l, j, 0, 0)),
            pl.BlockSpec((None, None, S5_ROW, S5_STATE), lambda l, j: (l, j, 0, 0)),
            pl.BlockSpec((None, None, S5_STATE, S5_ROW), lambda l, j: (l, j, 0, 0)),
            pl.BlockSpec((None, None, 2, S5_HALF), lambda l, j: (l, j, 0, 0)),
        ],
        out_shape=[
            jax.ShapeDtypeStruct((depth, nt, S5_ROW, S5_ROW), BF16),
            jax.ShapeDtypeStruct((depth, nt, S5_ROW, S5_STATE), BF16),
            jax.ShapeDtypeStruct((depth, nt, S5_STATE, S5_ROW), BF16),
            jax.ShapeDtypeStruct((depth, nt, 2, S5_HALF), F32),
        ],
        compiler_params=_params("parallel", "parallel"),
        name="s5_prep",
    )(b_layout(b_re), b_layout(b_im), c_layout(c_re), c_layout(c_im),
      lam_layout(lam_re), lam_layout(lam_im), ldt, s_lre, s_lim, s_ldt)


def _s5_kernel(u_ref, wt_ref, wl_ref, ws_ref, at_ref, d_ref, y_ref, loc_ref, st_ref, *, batch, chunks):
    x = u_ref[...]
    loc = jnp.dot(x, wl_ref[...], preferred_element_type=F32)
    half = S5_HALF // LANES
    for k in range(2 * half):
        loc_ref[k] = loc[:, k * LANES:(k + 1) * LANES]
    a_re = [at_ref[0:1, k * LANES:(k + 1) * LANES] for k in range(half)]
    a_im = [at_ref[1:2, k * LANES:(k + 1) * LANES] for k in range(half)]

    def body(c, carry):
        rows = pl.ds(c, batch, stride=chunks)
        new = [None] * (2 * half)
        for k in range(half):
            s_re, s_im = carry[k], carry[half + k]
            st_ref[k, rows, :] = s_re
            st_ref[half + k, rows, :] = s_im
            new[k] = a_re[k] * s_re - a_im[k] * s_im + loc_ref[k, rows, :]
            new[half + k] = a_re[k] * s_im + a_im[k] * s_re + loc_ref[half + k, rows, :]
        return tuple(new)

    zero = jnp.zeros((batch, LANES), F32)
    lax.fori_loop(0, chunks, body, (zero,) * (2 * half))

    y = jnp.dot(x, wt_ref[...], preferred_element_type=F32)
    st = jnp.concatenate([st_ref[k] for k in range(2 * half)], axis=-1).astype(BF16)
    y = y + jnp.dot(st, ws_ref[...], preferred_element_type=F32)
    y_ref[...] = (y + d_ref[...] * x.astype(F32)).astype(BF16)


def _s5(u, wt, wl, ws, at, d_rows, layer, batch, seq):
    nt, m, _ = u.shape
    chunks = seq // S5_T
    rows = m // S5_T
    u2 = u.reshape(nt, rows, S5_ROW)
    y2 = pl.pallas_call(
        functools.partial(_s5_kernel, batch=batch, chunks=chunks),
        grid=(nt,),
        in_specs=[
            pl.BlockSpec((None, rows, S5_ROW), lambda j: (j, 0, 0)),
            pl.BlockSpec((None, None, S5_ROW, S5_ROW), lambda j: (layer, j, 0, 0)),
            pl.BlockSpec((None, None, S5_ROW, S5_STATE), lambda j: (layer, j, 0, 0)),
            pl.BlockSpec((None, None, S5_STATE, S5_ROW), lambda j: (layer, j, 0, 0)),
            pl.BlockSpec((None, None, 2, S5_HALF), lambda j: (layer, j, 0, 0)),
            pl.BlockSpec((None, None, 1, S5_ROW), lambda j: (layer, j, 0, 0)),
        ],
        out_specs=pl.BlockSpec((None, rows, S5_ROW), lambda j: (j, 0, 0)),
        out_shape=jax.ShapeDtypeStruct((nt, rows, S5_ROW), BF16),
        scratch_shapes=[pltpu.VMEM((S5_STATE // LANES, rows, LANES), F32)] * 2,
        compiler_params=_params("parallel"),
        name="s5_mix",
    )(u2, wt, wl, ws, at, d_rows)
    return y2.reshape(nt, m, LANES)


def _out_kernel(y_ref, z_ref, x_ref, wglu_ref, wout_ref, gs_ref, g_ref, b_ref, o_ref, *, alpha):
    y = jnp.concatenate([y_ref[j] for j in range(y_ref.shape[0])], axis=-1).astype(F32)
    y = _gelu_tanh(y)
    y = y * _sigmoid(jnp.dot(y.astype(BF16), wglu_ref[...], preferred_element_type=F32))
    y = _rms_norm(y, gs_ref[...])
    yz = jnp.concatenate([y.astype(BF16), z_ref[...]], axis=-1)
    mix = jnp.dot(yz, wout_ref[...], preferred_element_type=F32)
    o_ref[...] = _layer_norm(alpha * x_ref[...] + mix, g_ref[...], b_ref[...])


def _out(y, z, x, w_glu, w_out, g_s5, ln_g, ln_b, layer, alpha, tm=512):
    m, d = x.shape
    nt = y.shape[0]
    d_s5 = nt * LANES
    d_conv = z.shape[1]
    return pl.pallas_call(
        functools.partial(_out_kernel, alpha=alpha),
        grid=(m // tm,),
        in_specs=[
            pl.BlockSpec((nt, tm, LANES), lambda i: (0, i, 0)),
            pl.BlockSpec((tm, d_conv), lambda i: (i, 0)),
            pl.BlockSpec((tm, d), lambda i: (i, 0)),
            pl.BlockSpec((None, d_s5, d_s5), lambda i: (layer, 0, 0)),
            pl.BlockSpec((None, d_s5 + d_conv, d), lambda i: (layer, 0, 0)),
            pl.BlockSpec((None, 1, d_s5), lambda i: (layer, 0, 0)),
            pl.BlockSpec((None, 1, d), lambda i: (layer, 0, 0)),
            pl.BlockSpec((None, 1, d), lambda i: (layer, 0, 0)),
        ],
        out_specs=pl.BlockSpec((tm, d), lambda i: (i, 0)),
        out_shape=jax.ShapeDtypeStruct((m, d), F32),
        compiler_params=_params("parallel"),
        name="glu_out",
    )(y, z, x, w_glu, w_out, g_s5, ln_g, ln_b)


def kernel(x, ffn1_gate, ffn1_up, ffn1_down, ln1_g, ln1_b, w_in, s5_lam_re, s5_lam_im, s5_log_dt,
           s5_b_re, s5_b_im, s5_c_re, s5_c_im, s5_d, s5_w_glu, conv_w, conv_b, g_s5, g_conv, w_out,
           ln2_g, ln2_b, ffn2_gate, ffn2_up, ffn2_down, ln3_g, ln3_b):
    batch, seq, d_model = x.shape
    depth = ffn1_gate.shape[0]
    d_s5 = s5_w_glu.shape[-1]
    alpha = (2.0 * depth) ** 0.25
    assert d_s5 % LANES == 0 and seq % S5_T == 0

    def row(a):
        return a.reshape(depth, 1, a.shape[-1])

    bf = lambda a: a.astype(BF16)
    ffn1 = (bf(ffn1_gate), bf(ffn1_up), bf(ffn1_down), row(ln1_g), row(ln1_b))
    ffn2 = (bf(ffn2_gate), bf(ffn2_up), bf(ffn2_down), row(ln3_g), row(ln3_b))
    w_in_b, w_glu_b, w_out_b = bf(w_in), bf(s5_w_glu), bf(w_out)

    wt, wl, ws, at = _s5_prep(s5_lam_re, s5_lam_im, s5_log_dt, s5_b_re, s5_b_im, s5_c_re, s5_c_im)
    nt = d_s5 // LANES
    d_rows = jnp.tile(s5_d.reshape(depth, nt, 1, LANES), (1, 1, 1, S5_T))

    h = x.reshape(batch * seq, d_model)
    for l in range(depth):
        h = _ffn(h, *ffn1, l, alpha)
        u, z = _proj(h, w_in_b, conv_w, row(conv_b), row(g_conv), l, seq, d_s5)
        y = _s5(u, wt, wl, ws, at, d_rows, l, batch, seq)
        h = _out(y, z, h, w_glu_b, w_out_b, row(g_s5), row(ln2_g), row(ln2_b), l, alpha)
        h = _ffn(h, *ffn2, l, alpha)
    return h.reshape(batch, seq, d_model)
```

```python
import functools

import jax
import jax.numpy as jnp
from jax import lax
from jax.experimental import pallas as pl
from jax.experimental.pallas import tpu as pltpu

F32 = jnp.float32
BF16 = jnp.bfloat16

LN_EPS = 1e-5
RMS_EPS = 1e-6
CONV_W = 3

LANES = 128
SUBLANES = 8
V7X_VMEM_BYTES = 64 * 1024 * 1024
VMEM_CAP_BYTES = V7X_VMEM_BYTES - 4 * 1024 * 1024
COMPILER_TEMP_BYTES = 8 * 1024 * 1024

S5_P = 16
S5_N = 64
S5_T = 8
S5_GPT = LANES // S5_P
S5_HALF = S5_GPT * S5_N
S5_STATE = 2 * S5_HALF
S5_ROW = S5_T * LANES


def _layer_norm(y, g, b):
    mu = jnp.mean(y, axis=-1, keepdims=True)
    yc = y - mu
    var = jnp.mean(yc * yc, axis=-1, keepdims=True)
    return yc * lax.rsqrt(var + LN_EPS) * g + b


def _rms_norm(y, g):
    return y * lax.rsqrt(jnp.mean(y * y, axis=-1, keepdims=True) + RMS_EPS) * g


def _sigmoid(x):
    return 1.0 / (1.0 + jnp.exp(-x))


def _gelu_tanh(x):
    c = 0.7978845608028654
    return 0.5 * x * (1.0 + jnp.tanh(c * (x + 0.044715 * (x * x * x))))


def _cmul(ar, ai, br, bi):
    return ar * br - ai * bi, ar * bi + ai * br


def _params(sem, est_bytes):
    limit = min(VMEM_CAP_BYTES, est_bytes + COMPILER_TEMP_BYTES)
    return pltpu.CompilerParams(dimension_semantics=sem, vmem_limit_bytes=limit)


def _ffn_kernel(x_ref, wg_ref, wu_ref, wd_ref, g_ref, b_ref, o_ref, xb_ref, *, alpha):
    f = pl.program_id(1)

    @pl.when(f == 0)
    def _():
        xb_ref[...] = x_ref[...].astype(BF16)
        o_ref[...] = jnp.zeros_like(o_ref)

    xb = xb_ref[...]
    gate = jnp.dot(xb, wg_ref[...].astype(BF16), preferred_element_type=F32)
    up = jnp.dot(xb, wu_ref[...].astype(BF16), preferred_element_type=F32)
    h = (gate * _sigmoid(gate)) * up
    o_ref[...] += jnp.dot(h.astype(BF16), wd_ref[...].astype(BF16), preferred_element_type=F32)

    @pl.when(f == pl.num_programs(1) - 1)
    def _():
        y = alpha * x_ref[...] + 0.5 * o_ref[...]
        o_ref[...] = _layer_norm(y, g_ref[...], b_ref[...])


def _ffn(x, wg, wu, wd, ln_g, ln_b, layer, alpha, tm=1024, tf=256):
    m, d = x.shape
    ff = wg.shape[-1]
    est = (2 * 2 * tm * d * 4 + 2 * 3 * d * tf * 4 + tm * d * 2
           + 3 * d * tf * 2 + 2 * tm * tf * 4 + tm * tf * 2)
    return pl.pallas_call(
        functools.partial(_ffn_kernel, alpha=alpha),
        grid=(m // tm, ff // tf),
        in_specs=[
            pl.BlockSpec((tm, d), lambda i, f: (i, 0)),
            pl.BlockSpec((None, d, tf), lambda i, f: (layer, 0, f)),
            pl.BlockSpec((None, d, tf), lambda i, f: (layer, 0, f)),
            pl.BlockSpec((None, tf, d), lambda i, f: (layer, f, 0)),
            pl.BlockSpec((None, 1, d), lambda i, f: (layer, 0, 0)),
            pl.BlockSpec((None, 1, d), lambda i, f: (layer, 0, 0)),
        ],
        out_specs=pl.BlockSpec((tm, d), lambda i, f: (i, 0)),
        out_shape=jax.ShapeDtypeStruct((m, d), F32),
        scratch_shapes=[pltpu.VMEM((tm, d), BF16)],
        compiler_params=_params(("parallel", "arbitrary"), est),
        name="ffn",
    )(x, wg, wu, wd, ln_g, ln_b)


def _proj_kernel(x_ref, win_ref, cw_ref, cb_ref, gc_ref, u_ref, z_ref, us_ref, vpad_ref, *, tiles_per_seq):
    i = pl.program_id(0)
    tm = x_ref.shape[0]
    nt = u_ref.shape[0]
    d_s5 = nt * LANES
    d_conv = z_ref.shape[1]
    xb = x_ref[...].astype(BF16)

    u = jnp.dot(xb, win_ref[:, 0:d_s5], preferred_element_type=F32)
    for j in range(nt):
        us_ref[j] = u[:, j * LANES:(j + 1) * LANES]
    for j in range(nt):
        for t in range(S5_T):
            rows = us_ref[j, pl.ds(t, tm // S5_T, stride=S5_T), :]
            u_ref[j, :, t * LANES:(t + 1) * LANES] = rows.astype(BF16)

    o = d_s5
    gate_c = jnp.dot(xb, win_ref[:, o + d_conv:o + 2 * d_conv], preferred_element_type=F32)
    hid = jnp.dot(xb, win_ref[:, o + 2 * d_conv:o + 3 * d_conv], preferred_element_type=F32)
    v = gate_c * hid

    @pl.when(i % tiles_per_seq == 0)
    def _():
        vpad_ref[0:SUBLANES, :] = jnp.zeros((SUBLANES, d_conv), F32)

    vpad_ref[SUBLANES:SUBLANES + tm, :] = v
    v1 = vpad_ref[SUBLANES - 1:SUBLANES - 1 + tm, :]
    v2 = vpad_ref[SUBLANES - 2:SUBLANES - 2 + tm, :]
    conv = cb_ref[...] + cw_ref[0:1, :] * v2 + cw_ref[1:2, :] * v1 + cw_ref[2:3, :] * v
    vpad_ref[0:SUBLANES, :] = vpad_ref[tm:tm + SUBLANES, :]

    gate_b = jnp.dot(xb, win_ref[:, o:o + d_conv], preferred_element_type=F32)
    z_ref[...] = _rms_norm(gate_b * conv, gc_ref[...]).astype(BF16)


def _proj(x, w_in, conv_w, conv_b, g_conv, layer, seq, d_s5, tm=256):
    m, d = x.shape
    d_in = w_in.shape[-1]
    d_conv = conv_b.shape[-1]
    nt = d_s5 // LANES
    assert CONV_W - 1 <= SUBLANES and conv_w.shape[1] == CONV_W and seq % tm == 0
    est = (2 * tm * d * 4 + 2 * d * d_in * 2 + 2 * tm * (d_s5 + d_conv) * 2
           + tm * d_s5 * 4 + (tm + SUBLANES) * d_conv * 4 + tm * d * 2 + 4 * tm * d_conv * 4)
    return pl.pallas_call(
        functools.partial(_proj_kernel, tiles_per_seq=seq // tm),
        grid=(m // tm,),
        in_specs=[
            pl.BlockSpec((tm, d), lambda i: (i, 0)),
            pl.BlockSpec((None, d, d_in), lambda i: (layer, 0, 0)),
            pl.BlockSpec((None, CONV_W, d_conv), lambda i: (layer, 0, 0)),
            pl.BlockSpec((None, 1, d_conv), lambda i: (layer, 0, 0)),
            pl.BlockSpec((None, 1, d_conv), lambda i: (layer, 0, 0)),
        ],
        out_specs=[
            pl.BlockSpec((nt, tm // S5_T, S5_ROW), lambda i: (0, i, 0)),
            pl.BlockSpec((tm, d_conv), lambda i: (i, 0)),
        ],
        out_shape=[
            jax.ShapeDtypeStruct((nt, m // S5_T, S5_ROW), BF16),
            jax.ShapeDtypeStruct((m, d_conv), BF16),
        ],
        scratch_shapes=[pltpu.VMEM((nt, tm, LANES), F32), pltpu.VMEM((tm + SUBLANES, d_conv), F32)],
        compiler_params=_params(("arbitrary",), est),
        name="proj_conv",
    )(x, w_in, conv_w, conv_b, g_conv)


def _discretize(lre, lim, ldt):
    dt = jnp.exp(ldt)
    mag = jnp.exp(lre * dt)
    ang = lim * dt
    return mag * jnp.cos(ang), mag * jnp.sin(ang)


def _s5_prep_kernel(bre_ref, bim_ref, cre_ref, cim_ref, lre_ref, lim_ref, ldt_ref,
                    slre_ref, slim_ref, sldt_ref, wt_ref, wl_ref, ws_ref, at_ref):
    lre, lim = lre_ref[...], lim_ref[...]
    ab_re, ab_im = _discretize(lre, lim, ldt_ref[...])
    den = lre * lre + lim * lim
    nr, ni = ab_re - 1.0, ab_im
    q_re = (nr * lre + ni * lim) / den
    q_im = (ni * lre - nr * lim) / den
    b_re, b_im = bre_ref[...], bim_ref[...]
    bb_re = q_re * b_re - q_im * b_im
    bb_im = q_re * b_im + q_im * b_re
    c_re, c_im = cre_ref[...], cim_ref[...]

    rows = lax.broadcasted_iota(jnp.int32, (S5_HALF, LANES), 0) // S5_N
    cols = lax.broadcasted_iota(jnp.int32, (S5_HALF, LANES), 1) // S5_P
    same_group = rows == cols
    same_group_sq = (lax.broadcasted_iota(jnp.int32, (LANES, LANES), 0) // S5_P
                     == lax.broadcasted_iota(jnp.int32, (LANES, LANES), 1) // S5_P)

    def blk(a):
        return jnp.where(same_group, jnp.concatenate([a] * S5_GPT, axis=0), 0.0)

    def dot_exact(a, b):
        return jnp.dot(a, b, precision=lax.Precision.HIGHEST, preferred_element_type=F32)

    p_re, p_im = jnp.ones_like(ab_re), jnp.zeros_like(ab_re)
    toeplitz = []
    for k in range(S5_T + 1):
        if k < S5_T:
            bl_re, bl_im = _cmul(p_re, p_im, bb_re, bb_im)
            tau = S5_T - 1 - k
            wl_ref[tau * LANES:(tau + 1) * LANES, 0:S5_HALF] = blk(bl_re).T.astype(BF16)
            wl_ref[tau * LANES:(tau + 1) * LANES, S5_HALF:S5_STATE] = blk(bl_im).T.astype(BF16)
            kk = dot_exact(bl_re.T, c_re) - dot_exact(bl_im.T, c_im)
            toeplitz.append(jnp.where(same_group_sq, kk, 0.0).astype(BF16))
        if k >= 1:
            cl_re, cl_im = _cmul(p_re, p_im, c_re, c_im)
            t = k - 1
            ws_ref[0:S5_HALF, t * LANES:(t + 1) * LANES] = blk(cl_re).astype(BF16)
            ws_ref[S5_HALF:S5_STATE, t * LANES:(t + 1) * LANES] = (-blk(cl_im)).astype(BF16)
        p_re, p_im = _cmul(p_re, p_im, ab_re, ab_im)

    zero = jnp.zeros((LANES, LANES), BF16)
    for tau in range(S5_T):
        for t in range(S5_T):
            wt_ref[tau * LANES:(tau + 1) * LANES, t * LANES:(t + 1) * LANES] = (
                toeplitz[t - tau] if t >= tau else zero)

    a_re, a_im = _discretize(slre_ref[...], slim_ref[...], sldt_ref[...])
    step = 1
    while step < S5_T:
        a_re, a_im = _cmul(a_re, a_im, a_re, a_im)
        step *= 2
    at_ref[0:1, :] = a_re
    at_ref[1:2, :] = a_im


def _s5_prep(lam_re, lam_im, log_dt, b_re, b_im, c_re, c_im):
    depth, g, n = lam_re.shape
    p = b_re.shape[-1]
    assert (p, n) == (S5_P, S5_N) and S5_T & (S5_T - 1) == 0
    nt = g // S5_GPT

    def b_layout(a):
        return a.reshape(depth, nt, S5_GPT, n, p).transpose(0, 1, 3, 2, 4).reshape(depth, nt, n, LANES)

    def c_layout(a):
        return a.reshape(depth, nt, S5_GPT, p, n).transpose(0, 1, 4, 2, 3).reshape(depth, nt, n, LANES)

    def lam_layout(a):
        a = a.reshape(depth, nt, S5_GPT, n).transpose(0, 1, 3, 2)
        return jnp.broadcast_to(a[..., None], (depth, nt, n, S5_GPT, p)).reshape(depth, nt, n, LANES)

    ldt = jnp.broadcast_to(log_dt.reshape(depth, nt, 1, S5_GPT, 1), (depth, nt, n, S5_GPT, p))
    ldt = ldt.reshape(depth, nt, n, LANES)
    s_lre = lam_re.reshape(depth, nt, 1, S5_HALF)
    s_lim = lam_im.reshape(depth, nt, 1, S5_HALF)
    s_ldt = jnp.broadcast_to(log_dt.reshape(depth, nt, S5_GPT, 1), (depth, nt, S5_GPT, n))
    s_ldt = s_ldt.reshape(depth, nt, 1, S5_HALF)

    tile_spec = pl.BlockSpec((None, None, n, LANES), lambda l, j: (l, j, 0, 0))
    lane_spec = pl.BlockSpec((None, None, 1, S5_HALF), lambda l, j: (l, j, 0, 0))
    est = 2 * 2 * (S5_ROW * S5_ROW + 2 * S5_ROW * S5_STATE)
    return pl.pallas_call(
        _s5_prep_kernel,
        grid=(depth, nt),
        in_specs=[tile_spec] * 7 + [lane_spec] * 3,
        out_specs=[
            pl.BlockSpec((None, None, S5_ROW, S5_ROW), lambda l, j: (l, j, 0, 0)),
            pl.BlockSpec((None, None, S5_ROW, S5_STATE), lambda l, j: (l, j, 0, 0)),
            pl.BlockSpec((None, None, S5_STATE, S5_ROW), lambda l, j: (l, j, 0, 0)),
            pl.BlockSpec((None, None, 2, S5_HALF), lambda l, j: (l, j, 0, 0)),
        ],
        out_shape=[
            jax.ShapeDtypeStruct((depth, nt, S5_ROW, S5_ROW), BF16),
            jax.ShapeDtypeStruct((depth, nt, S5_ROW, S5_STATE), BF16),
            jax.ShapeDtypeStruct((depth, nt, S5_STATE, S5_ROW), BF16),
            jax.ShapeDtypeStruct((depth, nt, 2, S5_HALF), F32),
        ],
        compiler_params=_params(("parallel", "parallel"), est),
        name="s5_prep",
    )(b_layout(b_re), b_layout(b_im), c_layout(c_re), c_layout(c_im),
      lam_layout(lam_re), lam_layout(lam_im), ldt, s_lre, s_lim, s_ldt)


def _s5_kernel(u_ref, wt_ref, wl_ref, ws_ref, at_ref, d_ref, y_ref, loc_ref, st_ref, *, batch, chunks, pitch):
    x = u_ref[...]
    loc = jnp.dot(x, wl_ref[...], preferred_element_type=F32)
    slabs = S5_STATE // LANES
    half = slabs // 2
    for k in range(slabs):
        for b in range(batch):
            loc_ref[k, b * pitch:b * pitch + chunks, :] = loc[b * chunks:(b + 1) * chunks, k * LANES:(k + 1) * LANES]
    a_re = [at_ref[0:1, k * LANES:(k + 1) * LANES] for k in range(half)]
    a_im = [at_ref[1:2, k * LANES:(k + 1) * LANES] for k in range(half)]

    def body(c, carry):
        rows = pl.ds(c, batch, stride=pitch)
        new = [None] * slabs
        for k in range(half):
            s_re, s_im = carry[k], carry[half + k]
            st_ref[k, rows, :] = s_re
            st_ref[half + k, rows, :] = s_im
            new[k] = a_re[k] * s_re - a_im[k] * s_im + loc_ref[k, rows, :]
            new[half + k] = a_re[k] * s_im + a_im[k] * s_re + loc_ref[half + k, rows, :]
        return tuple(new)

    zero = jnp.zeros((batch, LANES), F32)
    lax.fori_loop(0, chunks, body, (zero,) * slabs, unroll=2)

    y = jnp.dot(x, wt_ref[...], preferred_element_type=F32)
    st = jnp.concatenate(
        [jnp.concatenate([st_ref[k, b * pitch:b * pitch + chunks, :] for b in range(batch)], axis=0)
         for k in range(slabs)], axis=-1).astype(BF16)
    y = y + jnp.dot(st, ws_ref[...], preferred_element_type=F32)
    y_ref[...] = (y + d_ref[...] * x.astype(F32)).astype(BF16)


def _s5(u2, wt, wl, ws, at, d_rows, layer, batch, seq):
    nt, rows, _ = u2.shape
    chunks = seq // S5_T
    assert rows == batch * chunks and chunks % SUBLANES == 0
    pitch = chunks + SUBLANES if (chunks // SUBLANES) % 2 == 0 else chunks
    est = (2 * 2 * rows * S5_ROW * 2 + 2 * 2 * (S5_ROW * S5_ROW + 2 * S5_ROW * S5_STATE)
           + 2 * batch * pitch * S5_STATE * 4 + 3 * rows * S5_ROW * 4 + rows * S5_STATE * 2)
    return pl.pallas_call(
        functools.partial(_s5_kernel, batch=batch, chunks=chunks, pitch=pitch),
        grid=(nt,),
        in_specs=[
            pl.BlockSpec((None, rows, S5_ROW), lambda j: (j, 0, 0)),
            pl.BlockSpec((None, None, S5_ROW, S5_ROW), lambda j: (layer, j, 0, 0)),
            pl.BlockSpec((None, None, S5_ROW, S5_STATE), lambda j: (layer, j, 0, 0)),
            pl.BlockSpec((None, None, S5_STATE, S5_ROW), lambda j: (layer, j, 0, 0)),
            pl.BlockSpec((None, None, 2, S5_HALF), lambda j: (layer, j, 0, 0)),
            pl.BlockSpec((None, None, 1, S5_ROW), lambda j: (layer, j, 0, 0)),
        ],
        out_specs=pl.BlockSpec((None, rows, S5_ROW), lambda j: (j, 0, 0)),
        out_shape=jax.ShapeDtypeStruct((nt, rows, S5_ROW), BF16),
        scratch_shapes=[pltpu.VMEM((S5_STATE // LANES, batch * pitch, LANES), F32)] * 2,
        compiler_params=_params(("parallel",), est),
        name="s5_mix",
    )(u2, wt, wl, ws, at, d_rows)


def _out_kernel(y_ref, z_ref, x_ref, wglu_ref, wout_ref, gs_ref, g_ref, b_ref, o_ref, ys_ref, *, alpha):
    nt = y_ref.shape[0]
    tm = x_ref.shape[0]
    for j in range(nt):
        for t in range(S5_T):
            ys_ref[j, pl.ds(t, tm // S5_T, stride=S5_T), :] = y_ref[j, :, t * LANES:(t + 1) * LANES].astype(F32)
    y = jnp.concatenate([ys_ref[j] for j in range(nt)], axis=-1)
    y = _gelu_tanh(y)
    y = y * _sigmoid(jnp.dot(y.astype(BF16), wglu_ref[...], preferred_element_type=F32))
    y = _rms_norm(y, gs_ref[...])
    yz = jnp.concatenate([y.astype(BF16), z_ref[...]], axis=-1)
    mix = jnp.dot(yz, wout_ref[...], preferred_element_type=F32)
    o_ref[...] = _layer_norm(alpha * x_ref[...] + mix, g_ref[...], b_ref[...])


def _out(y2, z, x, w_glu, w_out, g_s5, ln_g, ln_b, layer, alpha, tm=512):
    m, d = x.shape
    nt = y2.shape[0]
    d_s5 = nt * LANES
    d_conv = z.shape[1]
    est = (2 * 2 * tm * d * 4 + 2 * tm * (d_s5 + d_conv) * 2 + 2 * (d_s5 * d_s5 + (d_s5 + d_conv) * d) * 2
           + tm * d_s5 * 4 + 3 * tm * d_s5 * 4 + tm * d * 4)
    return pl.pallas_call(
        functools.partial(_out_kernel, alpha=alpha),
        grid=(m // tm,),
        in_specs=[
            pl.BlockSpec((nt, tm // S5_T, S5_ROW), lambda i: (0, i, 0)),
            pl.BlockSpec((tm, d_conv), lambda i: (i, 0)),
            pl.BlockSpec((tm, d), lambda i: (i, 0)),
            pl.BlockSpec((None, d_s5, d_s5), lambda i: (layer, 0, 0)),
            pl.BlockSpec((None, d_s5 + d_conv, d), lambda i: (layer, 0, 0)),
            pl.BlockSpec((None, 1, d_s5), lambda i: (layer, 0, 0)),
            pl.BlockSpec((None, 1, d), lambda i: (layer, 0, 0)),
            pl.BlockSpec((None, 1, d), lambda i: (layer, 0, 0)),
        ],
        out_specs=pl.BlockSpec((tm, d), lambda i: (i, 0)),
        out_shape=jax.ShapeDtypeStruct((m, d), F32),
        scratch_shapes=[pltpu.VMEM((nt, tm, LANES), F32)],
        compiler_params=_params(("parallel",), est),
        name="glu_out",
    )(y2, z, x, w_glu, w_out, g_s5, ln_g, ln_b)


def kernel(x, ffn1_gate, ffn1_up, ffn1_down, ln1_g, ln1_b, w_in, s5_lam_re, s5_lam_im, s5_log_dt,
           s5_b_re, s5_b_im, s5_c_re, s5_c_im, s5_d, s5_w_glu, conv_w, conv_b, g_s5, g_conv, w_out,
           ln2_g, ln2_b, ffn2_gate, ffn2_up, ffn2_down, ln3_g, ln3_b):
    batch, seq, d_model = x.shape
    depth = ffn1_gate.shape[0]
    d_s5 = s5_w_glu.shape[-1]
    alpha = (2.0 * depth) ** 0.25
    assert d_s5 % LANES == 0 and seq % S5_T == 0

    def row(a):
        return a.reshape(depth, 1, a.shape[-1])

    ffn1 = (ffn1_gate, ffn1_up, ffn1_down, row(ln1_g), row(ln1_b))
    ffn2 = (ffn2_gate, ffn2_up, ffn2_down, row(ln3_g), row(ln3_b))
    w_in_b, w_glu_b, w_out_b = w_in.astype(BF16), s5_w_glu.astype(BF16), w_out.astype(BF16)

    wt, wl, ws, at = _s5_prep(s5_lam_re, s5_lam_im, s5_log_dt, s5_b_re, s5_b_im, s5_c_re, s5_c_im)
    nt = d_s5 // LANES
    d_rows = jnp.tile(s5_d.reshape(depth, nt, 1, LANES), (1, 1, 1, S5_T))

    h = x.reshape(batch * seq, d_model)
    for l in range(depth):
        h = _ffn(h, *ffn1, l, alpha)
        u2, z = _proj(h, w_in_b, conv_w, row(conv_b), row(g_conv), l, seq, d_s5)
        y2 = _s5(u2, wt, wl, ws, at, d_rows, l, batch, seq)
        h = _out(y2, z, h, w_glu_b, w_out_b, row(g_s5), row(ln2_g), row(ln2_b), l, alpha)
        h = _ffn(h, *ffn2, l, alpha)
    return h.reshape(batch, seq, d_model)
```

```python
import functools

import jax
import jax.numpy as jnp
from jax import lax
from jax.experimental import pallas as pl
from jax.experimental.pallas import tpu as pltpu

F32 = jnp.float32
BF16 = jnp.bfloat16

LN_EPS = 1e-5
RMS_EPS = 1e-6
CONV_W = 3
FFN_BETA = 0.5

LANES = 128
SUBLANES = 8
V7X_VMEM_BYTES = 64 * 1024 * 1024
VMEM_CAP_BYTES = V7X_VMEM_BYTES - 4 * 1024 * 1024
COMPILER_TEMP_BYTES = 8 * 1024 * 1024

S5_P = 16
S5_N = 64
S5_T = 8
S5_GPT = LANES // S5_P
S5_HALF = S5_GPT * S5_N
S5_STATE = 2 * S5_HALF
S5_ROW = S5_T * LANES


def _layer_norm(y, g, b, eps=LN_EPS):
    mu = jnp.mean(y, axis=-1, keepdims=True)
    yc = y - mu
    var = jnp.mean(yc * yc, axis=-1, keepdims=True)
    return yc * lax.rsqrt(var + eps) * g + b


def _rms_norm(y, g):
    return y * lax.rsqrt(jnp.mean(y * y, axis=-1, keepdims=True) + RMS_EPS) * g


def _sigmoid(x):
    return 1.0 / (1.0 + jnp.exp(-x))


def _gelu_tanh(x):
    c = 0.7978845608028654
    return 0.5 * x * (1.0 + jnp.tanh(c * (x + 0.044715 * (x * x * x))))


def _cmul(ar, ai, br, bi):
    return ar * br - ai * bi, ar * bi + ai * br


def _params(sem, est_bytes):
    limit = min(VMEM_CAP_BYTES, est_bytes + COMPILER_TEMP_BYTES)
    return pltpu.CompilerParams(dimension_semantics=sem, vmem_limit_bytes=limit)


def _ffn_kernel(x_ref, wg_ref, wu_ref, wd_ref, g_ref, b_ref, o_ref, xb_ref, *, alpha, row_blocks):
    f = pl.program_id(1)
    last = pl.num_programs(1) - 1

    def weights():
        return wg_ref[...].astype(BF16), wu_ref[...].astype(BF16), wd_ref[...].astype(BF16)

    def swiglu_down(xb, wg, wu, wd):
        gate = jnp.dot(xb, wg, preferred_element_type=F32)
        up = jnp.dot(xb, wu, preferred_element_type=F32)
        h = (gate * _sigmoid(gate)) * up
        return jnp.dot(h.astype(BF16), wd, preferred_element_type=F32)

    @pl.when(f == 0)
    def _():
        x = x_ref[...]
        xb = x.astype(BF16)
        xb_ref[...] = xb
        o_ref[...] = (alpha / FFN_BETA) * x + swiglu_down(xb, *weights())

    @pl.when(jnp.logical_and(f > 0, f < last))
    def _():
        o_ref[...] += swiglu_down(xb_ref[...], *weights())

    @pl.when(f == last)
    def _():
        w = weights()
        rb = x_ref.shape[0] // row_blocks
        for r in range(row_blocks):
            rows = slice(r * rb, (r + 1) * rb)
            acc = o_ref[rows, :] + swiglu_down(xb_ref[rows, :], *w)
            o_ref[rows, :] = _layer_norm(acc, g_ref[...], b_ref[...], LN_EPS / (FFN_BETA * FFN_BETA))


def _ffn(x, wg, wu, wd, ln_g, ln_b, layer, alpha, tm=1024, tf=256, row_blocks=4):
    m, d = x.shape
    ff = wg.shape[-1]
    assert ff // tf >= 2 and tm % (row_blocks * 2 * SUBLANES) == 0
    est = (2 * 2 * tm * d * 4 + 2 * 3 * d * tf * 4 + tm * d * 2
           + 3 * d * tf * 2 + 2 * tm * tf * 4 + tm * tf * 2)
    return pl.pallas_call(
        functools.partial(_ffn_kernel, alpha=alpha, row_blocks=row_blocks),
        grid=(m // tm, ff // tf),
        in_specs=[
            pl.BlockSpec((tm, d), lambda i, f: (i, 0)),
            pl.BlockSpec((None, d, tf), lambda i, f: (layer, 0, f)),
            pl.BlockSpec((None, d, tf), lambda i, f: (layer, 0, f)),
            pl.BlockSpec((None, tf, d), lambda i, f: (layer, f, 0)),
            pl.BlockSpec((None, 1, d), lambda i, f: (layer, 0, 0)),
            pl.BlockSpec((None, 1, d), lambda i, f: (layer, 0, 0)),
        ],
        out_specs=pl.BlockSpec((tm, d), lambda i, f: (i, 0)),
        out_shape=jax.ShapeDtypeStruct((m, d), F32),
        scratch_shapes=[pltpu.VMEM((tm, d), BF16)],
        compiler_params=_params(("parallel", "arbitrary"), est),
        name="ffn",
    )(x, wg, wu, wd, ln_g, ln_b)


def _proj_kernel(x_ref, win_ref, cw_ref, cb_ref, gc_ref, u_ref, z_ref, us_ref, vpad_ref, *,
                 tiles_per_seq, row_blocks):
    tm = x_ref.shape[0]
    nt = u_ref.shape[0]
    d_s5 = nt * LANES
    d_conv = z_ref.shape[1]
    rb = tm // row_blocks
    cb = rb // S5_T

    @pl.when(pl.program_id(0) % tiles_per_seq == 0)
    def _():
        vpad_ref[0:SUBLANES, :] = jnp.zeros((SUBLANES, d_conv), F32)

    for r in range(row_blocks):
        r0 = r * rb
        xb = x_ref[r0:r0 + rb, :].astype(BF16)

        u = jnp.dot(xb, win_ref[:, 0:d_s5], preferred_element_type=F32)
        for j in range(nt):
            us_ref[j, r0:r0 + rb, :] = u[:, j * LANES:(j + 1) * LANES]
        for j in range(nt):
            for t in range(S5_T):
                rows = us_ref[j, pl.ds(r0 + t, cb, stride=S5_T), :]
                u_ref[j, r * cb:(r + 1) * cb, t * LANES:(t + 1) * LANES] = rows.astype(BF16)

        o = d_s5
        gate_c = jnp.dot(xb, win_ref[:, o + d_conv:o + 2 * d_conv], preferred_element_type=F32)
        hid = jnp.dot(xb, win_ref[:, o + 2 * d_conv:o + 3 * d_conv], preferred_element_type=F32)
        v = gate_c * hid
        p0 = SUBLANES + r0
        vpad_ref[p0:p0 + rb, :] = v
        v1 = vpad_ref[p0 - 1:p0 - 1 + rb, :]
        v2 = vpad_ref[p0 - 2:p0 - 2 + rb, :]
        conv = cb_ref[...] + cw_ref[0:1, :] * v2 + cw_ref[1:2, :] * v1 + cw_ref[2:3, :] * v

        gate_b = jnp.dot(xb, win_ref[:, o:o + d_conv], preferred_element_type=F32)
        z_ref[r0:r0 + rb, :] = _rms_norm(gate_b * conv, gc_ref[...]).astype(BF16)

    vpad_ref[0:SUBLANES, :] = vpad_ref[tm:tm + SUBLANES, :]


def _proj(x, w_in, conv_w, conv_b, g_conv, layer, seq, d_s5, tm=512, row_blocks=2):
    m, d = x.shape
    d_in = w_in.shape[-1]
    d_conv = conv_b.shape[-1]
    nt = d_s5 // LANES
    assert CONV_W - 1 <= SUBLANES and conv_w.shape[1] == CONV_W and seq % tm == 0
    assert tm % (row_blocks * S5_T * 2 * SUBLANES) == 0
    rb = tm // row_blocks
    est = (2 * tm * d * 4 + d * d_in * 2 + 2 * tm * (d_s5 + d_conv) * 2
           + tm * d_s5 * 4 + (tm + SUBLANES) * d_conv * 4 + rb * d * 2 + 6 * rb * d_conv * 4)
    return pl.pallas_call(
        functools.partial(_proj_kernel, tiles_per_seq=seq // tm, row_blocks=row_blocks),
        grid=(m // tm,),
        in_specs=[
            pl.BlockSpec((tm, d), lambda i: (i, 0)),
            pl.BlockSpec((None, d, d_in), lambda i: (layer, 0, 0), pipeline_mode=pl.Buffered(1)),
            pl.BlockSpec((None, CONV_W, d_conv), lambda i: (layer, 0, 0)),
            pl.BlockSpec((None, 1, d_conv), lambda i: (layer, 0, 0)),
            pl.BlockSpec((None, 1, d_conv), lambda i: (layer, 0, 0)),
        ],
        out_specs=[
            pl.BlockSpec((nt, tm // S5_T, S5_ROW), lambda i: (0, i, 0)),
            pl.BlockSpec((tm, d_conv), lambda i: (i, 0)),
        ],
        out_shape=[
            jax.ShapeDtypeStruct((nt, m // S5_T, S5_ROW), BF16),
            jax.ShapeDtypeStruct((m, d_conv), BF16),
        ],
        scratch_shapes=[pltpu.VMEM((nt, tm, LANES), F32), pltpu.VMEM((tm + SUBLANES, d_conv), F32)],
        compiler_params=_params(("arbitrary",), est),
        name="proj_conv",
    )(x, w_in, conv_w, conv_b, g_conv)


def _discretize(lre, lim, ldt):
    dt = jnp.exp(ldt)
    mag = jnp.exp(lre * dt)
    ang = lim * dt
    return mag * jnp.cos(ang), mag * jnp.sin(ang)


def _s5_prep_kernel(bre_ref, bim_ref, cre_ref, cim_ref, lre_ref, lim_ref, ldt_ref,
                    slre_ref, slim_ref, sldt_ref, wt_ref, wl_ref, ws_ref, at_ref):
    lre, lim = lre_ref[...], lim_ref[...]
    ab_re, ab_im = _discretize(lre, lim, ldt_ref[...])
    den = lre * lre + lim * lim
    nr, ni = ab_re - 1.0, ab_im
    q_re = (nr * lre + ni * lim) / den
    q_im = (ni * lre - nr * lim) / den
    b_re, b_im = bre_ref[...], bim_ref[...]
    bb_re = q_re * b_re - q_im * b_im
    bb_im = q_re * b_im + q_im * b_re
    c_re, c_im = cre_ref[...], cim_ref[...]

    rows = lax.broadcasted_iota(jnp.int32, (S5_HALF, LANES), 0) // S5_N
    cols = lax.broadcasted_iota(jnp.int32, (S5_HALF, LANES), 1) // S5_P
    same_group = rows == cols
    same_group_sq = (lax.broadcasted_iota(jnp.int32, (LANES, LANES), 0) // S5_P
                     == lax.broadcasted_iota(jnp.int32, (LANES, LANES), 1) // S5_P)

    def blk(a):
        return jnp.where(same_group, jnp.concatenate([a] * S5_GPT, axis=0), 0.0)

    def dot_exact(a, b):
        return jnp.dot(a, b, precision=lax.Precision.HIGHEST, preferred_element_type=F32)

    p_re, p_im = jnp.ones_like(ab_re), jnp.zeros_like(ab_re)
    toeplitz = []
    for k in range(S5_T + 1):
        if k < S5_T:
            bl_re, bl_im = _cmul(p_re, p_im, bb_re, bb_im)
            tau = S5_T - 1 - k
            wl_ref[tau * LANES:(tau + 1) * LANES, 0:S5_HALF] = blk(bl_re).T.astype(BF16)
            wl_ref[tau * LANES:(tau + 1) * LANES, S5_HALF:S5_STATE] = blk(bl_im).T.astype(BF16)
            kk = dot_exact(bl_re.T, c_re) - dot_exact(bl_im.T, c_im)
            toeplitz.append(jnp.where(same_group_sq, kk, 0.0).astype(BF16))
        if k >= 1:
            cl_re, cl_im = _cmul(p_re, p_im, c_re, c_im)
            t = k - 1
            ws_ref[0:S5_HALF, t * LANES:(t + 1) * LANES] = blk(cl_re).astype(BF16)
            ws_ref[S5_HALF:S5_STATE, t * LANES:(t + 1) * LANES] = (-blk(cl_im)).astype(BF16)
        p_re, p_im = _cmul(p_re, p_im, ab_re, ab_im)

    zero = jnp.zeros((LANES, LANES), BF16)
    for tau in range(S5_T):
        for t in range(S5_T):
            wt_ref[tau * LANES:(tau + 1) * LANES, t * LANES:(t + 1) * LANES] = (
                toeplitz[t - tau] if t >= tau else zero)

    a_re, a_im = _discretize(slre_ref[...], slim_ref[...], sldt_ref[...])
    step = 1
    while step < S5_T:
        a_re, a_im = _cmul(a_re, a_im, a_re, a_im)
        step *= 2
    at_ref[0:1, :] = a_re
    at_ref[1:2, :] = a_im


def _s5_prep(lam_re, lam_im, log_dt, b_re, b_im, c_re, c_im):
    depth, g, n = lam_re.shape
    p = b_re.shape[-1]
    assert (p, n) == (S5_P, S5_N) and S5_T & (S5_T - 1) == 0
    nt = g // S5_GPT

    def b_layout(a):
        return a.reshape(depth, nt, S5_GPT, n, p).transpose(0, 1, 3, 2, 4).reshape(depth, nt, n, LANES)

    def c_layout(a):
        return a.reshape(depth, nt, S5_GPT, p, n).transpose(0, 1, 4, 2, 3).reshape(depth, nt, n, LANES)

    def lam_layout(a):
        a = a.reshape(depth, nt, S5_GPT, n).transpose(0, 1, 3, 2)
        return jnp.broadcast_to(a[..., None], (depth, nt, n, S5_GPT, p)).reshape(depth, nt, n, LANES)

    ldt = jnp.broadcast_to(log_dt.reshape(depth, nt, 1, S5_GPT, 1), (depth, nt, n, S5_GPT, p))
    ldt = ldt.reshape(depth, nt, n, LANES)
    s_lre = lam_re.reshape(depth, nt, 1, S5_HALF)
    s_lim = lam_im.reshape(depth, nt, 1, S5_HALF)
    s_ldt = jnp.broadcast_to(log_dt.reshape(depth, nt, S5_GPT, 1), (depth, nt, S5_GPT, n))
    s_ldt = s_ldt.reshape(depth, nt, 1, S5_HALF)

    tile_spec = pl.BlockSpec((None, None, n, LANES), lambda l, j: (l, j, 0, 0))
    lane_spec = pl.BlockSpec((None, None, 1, S5_HALF), lambda l, j: (l, j, 0, 0))
    est = 2 * 2 * (S5_ROW * S5_ROW + 2 * S5_ROW * S5_STATE)
    return pl.pallas_call(
        _s5_prep_kernel,
        grid=(depth, nt),
        in_specs=[tile_spec] * 7 + [lane_spec] * 3,
        out_specs=[
            pl.BlockSpec((None, None, S5_ROW, S5_ROW), lambda l, j: (l, j, 0, 0)),
            pl.BlockSpec((None, None, S5_ROW, S5_STATE), lambda l, j: (l, j, 0, 0)),
            pl.BlockSpec((None, None, S5_STATE, S5_ROW), lambda l, j: (l, j, 0, 0)),
            pl.BlockSpec((None, None, 2, S5_HALF), lambda l, j: (l, j, 0, 0)),
        ],
        out_shape=[
            jax.ShapeDtypeStruct((depth, nt, S5_ROW, S5_ROW), BF16),
            jax.ShapeDtypeStruct((depth, nt, S5_ROW, S5_STATE), BF16),
            jax.ShapeDtypeStruct((depth, nt, S5_STATE, S5_ROW), BF16),
            jax.ShapeDtypeStruct((depth, nt, 2, S5_HALF), F32),
        ],
        compiler_params=_params(("parallel", "parallel"), est),
        name="s5_prep",
    )(b_layout(b_re), b_layout(b_im), c_layout(c_re), c_layout(c_im),
      lam_layout(lam_re), lam_layout(lam_im), ldt, s_lre, s_lim, s_ldt)


def _s5_kernel(u_ref, wt_ref, wl_ref, ws_ref, at_ref, d_ref, y_ref, loc_ref, st_ref, *, batch, chunks, pitch):
    x = u_ref[...]
    loc = jnp.dot(x, wl_ref[...], preferred_element_type=F32)
    slabs = S5_STATE // LANES
    half = slabs // 2
    for k in range(slabs):
        for b in range(batch):
            loc_ref[k, b * pitch:b * pitch + chunks, :] = loc[b * chunks:(b + 1) * chunks, k * LANES:(k + 1) * LANES]
    a_re = [at_ref[0:1, k * LANES:(k + 1) * LANES] for k in range(half)]
    a_im = [at_ref[1:2, k * LANES:(k + 1) * LANES] for k in range(half)]

    def body(c, carry):
        rows = pl.ds(c, batch, stride=pitch)
        new = [None] * slabs
        for k in range(half):
            s_re, s_im = carry[k], carry[half + k]
            st_ref[k, rows, :] = s_re
            st_ref[half + k, rows, :] = s_im
            new[k] = a_re[k] * s_re - a_im[k] * s_im + loc_ref[k, rows, :]
            new[half + k] = a_re[k] * s_im + a_im[k] * s_re + loc_ref[half + k, rows, :]
        return tuple(new)

    zero = jnp.zeros((batch, LANES), F32)
    lax.fori_loop(0, chunks, body, (zero,) * slabs, unroll=2)

    y = jnp.dot(x, wt_ref[...], preferred_element_type=F32)
    st = jnp.concatenate(
        [jnp.concatenate([st_ref[k, b * pitch:b * pitch + chunks, :] for b in range(batch)], axis=0)
         for k in range(slabs)], axis=-1).astype(BF16)
    y = y + jnp.dot(st, ws_ref[...], preferred_element_type=F32)
    y_ref[...] = (y + d_ref[...] * x.astype(F32)).astype(BF16)


def _s5(u2, wt, wl, ws, at, d_rows, layer, batch, seq):
    nt, rows, _ = u2.shape
    chunks = seq // S5_T
    assert rows == batch * chunks and chunks % SUBLANES == 0
    pitch = chunks + SUBLANES if (chunks // SUBLANES) % 2 == 0 else chunks
    est = (2 * 2 * rows * S5_ROW * 2 + 2 * 2 * (S5_ROW * S5_ROW + 2 * S5_ROW * S5_STATE)
           + 2 * batch * pitch * S5_STATE * 4 + 3 * rows * S5_ROW * 4 + rows * S5_STATE * 2)
    return pl.pallas_call(
        functools.partial(_s5_kernel, batch=batch, chunks=chunks, pitch=pitch),
        grid=(nt,),
        in_specs=[
            pl.BlockSpec((None, rows, S5_ROW), lambda j: (j, 0, 0)),
            pl.BlockSpec((None, None, S5_ROW, S5_ROW), lambda j: (layer, j, 0, 0)),
            pl.BlockSpec((None, None, S5_ROW, S5_STATE), lambda j: (layer, j, 0, 0)),
            pl.BlockSpec((None, None, S5_STATE, S5_ROW), lambda j: (layer, j, 0, 0)),
            pl.BlockSpec((None, None, 2, S5_HALF), lambda j: (layer, j, 0, 0)),
            pl.BlockSpec((None, None, 1, S5_ROW), lambda j: (layer, j, 0, 0)),
        ],
        out_specs=pl.BlockSpec((None, rows, S5_ROW), lambda j: (j, 0, 0)),
        out_shape=jax.ShapeDtypeStruct((nt, rows, S5_ROW), BF16),
        scratch_shapes=[pltpu.VMEM((S5_STATE // LANES, batch * pitch, LANES), F32)] * 2,
        compiler_params=_params(("parallel",), est),
        name="s5_mix",
    )(u2, wt, wl, ws, at, d_rows)


def _out_kernel(y_ref, z_ref, x_ref, wglu_ref, wout_ref, gs_ref, g_ref, b_ref, o_ref, ys_ref, *, alpha):
    nt = y_ref.shape[0]
    tm = x_ref.shape[0]
    for j in range(nt):
        for t in range(S5_T):
            ys_ref[j, pl.ds(t, tm // S5_T, stride=S5_T), :] = y_ref[j, :, t * LANES:(t + 1) * LANES].astype(F32)
    y = jnp.concatenate([ys_ref[j] for j in range(nt)], axis=-1)
    y = _gelu_tanh(y)
    y = y * _sigmoid(jnp.dot(y.astype(BF16), wglu_ref[...], preferred_element_type=F32))
    y = _rms_norm(y, gs_ref[...])
    yz = jnp.concatenate([y.astype(BF16), z_ref[...]], axis=-1)
    mix = jnp.dot(yz, wout_ref[...], preferred_element_type=F32)
    o_ref[...] = _layer_norm(alpha * x_ref[...] + mix, g_ref[...], b_ref[...])


def _out(y2, z, x, w_glu, w_out, g_s5, ln_g, ln_b, layer, alpha, tm=512):
    m, d = x.shape
    nt = y2.shape[0]
    d_s5 = nt * LANES
    d_conv = z.shape[1]
    assert tm % (S5_T * 2 * SUBLANES) == 0
    est = (2 * 2 * tm * d * 4 + 2 * tm * (d_s5 + d_conv) * 2 + (d_s5 * d_s5 + (d_s5 + d_conv) * d) * 2
           + tm * d_s5 * 4 + 3 * tm * d_s5 * 4 + tm * d * 4)
    return pl.pallas_call(
        functools.partial(_out_kernel, alpha=alpha),
        grid=(m // tm,),
        in_specs=[
            pl.BlockSpec((nt, tm // S5_T, S5_ROW), lambda i: (0, i, 0)),
            pl.BlockSpec((tm, d_conv), lambda i: (i, 0)),
            pl.BlockSpec((tm, d), lambda i: (i, 0)),
            pl.BlockSpec((None, d_s5, d_s5), lambda i: (layer, 0, 0), pipeline_mode=pl.Buffered(1)),
            pl.BlockSpec((None, d_s5 + d_conv, d), lambda i: (layer, 0, 0), pipeline_mode=pl.Buffered(1)),
            pl.BlockSpec((None, 1, d_s5), lambda i: (layer, 0, 0)),
            pl.BlockSpec((None, 1, d), lambda i: (layer, 0, 0)),
            pl.BlockSpec((None, 1, d), lambda i: (layer, 0, 0)),
        ],
        out_specs=pl.BlockSpec((tm, d), lambda i: (i, 0)),
        out_shape=jax.ShapeDtypeStruct((m, d), F32),
        scratch_shapes=[pltpu.VMEM((nt, tm, LANES), F32)],
        compiler_params=_params(("parallel",), est),
        name="glu_out",
    )(y2, z, x, w_glu, w_out, g_s5, ln_g, ln_b)


def kernel(x, ffn1_gate, ffn1_up, ffn1_down, ln1_g, ln1_b, w_in, s5_lam_re, s5_lam_im, s5_log_dt,
           s5_b_re, s5_b_im, s5_c_re, s5_c_im, s5_d, s5_w_glu, conv_w, conv_b, g_s5, g_conv, w_out,
           ln2_g, ln2_b, ffn2_gate, ffn2_up, ffn2_down, ln3_g, ln3_b):
    batch, seq, d_model = x.shape
    depth = ffn1_gate.shape[0]
    d_s5 = s5_w_glu.shape[-1]
    alpha = (2.0 * depth) ** 0.25
    assert d_s5 % LANES == 0 and seq % S5_T == 0

    def row(a):
        return a.reshape(depth, 1, a.shape[-1])

    ffn1 = (ffn1_gate, ffn1_up, ffn1_down, row(ln1_g), row(ln1_b))
    ffn2 = (ffn2_gate, ffn2_up, ffn2_down, row(ln3_g), row(ln3_b))
    w_in_b, w_glu_b, w_out_b = w_in.astype(BF16), s5_w_glu.astype(BF16), w_out.astype(BF16)

    wt, wl, ws, at = _s5_prep(s5_lam_re, s5_lam_im, s5_log_dt, s5_b_re, s5_b_im, s5_c_re, s5_c_im)
    nt = d_s5 // LANES
    d_rows = jnp.tile(s5_d.reshape(depth, nt, 1, LANES), (1, 1, 1, S5_T))

    h = x.reshape(batch * seq, d_model)
    for l in range(depth):
        h = _ffn(h, *ffn1, l, alpha)
        u2, z = _proj(h, w_in_b, conv_w, row(conv_b), row(g_conv), l, seq, d_s5)
        y2 = _s5(u2, wt, wl, ws, at, d_rows, l, batch, seq)
        h = _out(y2, z, h, w_glu_b, w_out_b, row(g_s5), row(ln2_g), row(ln2_b), l, alpha)
        h = _ffn(h, *ffn2, l, alpha)
    return h.reshape(batch, seq, d_model)
```

```python
import functools

import jax
import jax.numpy as jnp
from jax import lax
from jax.experimental import pallas as pl
from jax.experimental.pallas import tpu as pltpu

F32 = jnp.float32
BF16 = jnp.bfloat16

LN_EPS = 1e-5
RMS_EPS = 1e-6
CONV_W = 3
FFN_BETA = 0.5

LANES = 128
SUBLANES = 8
V7X_VMEM_BYTES = 64 * 1024 * 1024
VMEM_CAP_BYTES = V7X_VMEM_BYTES - 4 * 1024 * 1024
COMPILER_TEMP_BYTES = 8 * 1024 * 1024

S5_P = 16
S5_N = 64
S5_T = 8
S5_GPT = LANES // S5_P
S5_HALF = S5_GPT * S5_N
S5_STATE = 2 * S5_HALF
S5_ROW = S5_T * LANES


def _layer_norm(y, g, b, eps=LN_EPS):
    mu = jnp.mean(y, axis=-1, keepdims=True)
    yc = y - mu
    var = jnp.mean(yc * yc, axis=-1, keepdims=True)
    return yc * lax.rsqrt(var + eps) * g + b


def _rms_norm(y, g):
    return y * lax.rsqrt(jnp.mean(y * y, axis=-1, keepdims=True) + RMS_EPS) * g


def _sigmoid(x):
    return 1.0 / (1.0 + jnp.exp(-x))


def _gelu_tanh(x):
    c = 0.7978845608028654
    return 0.5 * x * (1.0 + jnp.tanh(c * (x + 0.044715 * (x * x * x))))


def _cmul(ar, ai, br, bi):
    return ar * br - ai * bi, ar * bi + ai * br


def _params(sem, est_bytes):
    limit = min(VMEM_CAP_BYTES, est_bytes + COMPILER_TEMP_BYTES)
    return pltpu.CompilerParams(dimension_semantics=sem, vmem_limit_bytes=limit)


def _ffn_kernel(x_ref, wg_ref, wu_ref, wd_ref, g_ref, b_ref, o_ref, xb_ref, *, alpha, row_blocks):
    f = pl.program_id(1)
    last = pl.num_programs(1) - 1

    def weights():
        return wg_ref[...].astype(BF16), wu_ref[...].astype(BF16), wd_ref[...].astype(BF16)

    def swiglu_down(xb, wg, wu, wd):
        gate = jnp.dot(xb, wg, preferred_element_type=F32)
        up = jnp.dot(xb, wu, preferred_element_type=F32)
        h = (gate * _sigmoid(gate)) * up
        return jnp.dot(h.astype(BF16), wd, preferred_element_type=F32)

    @pl.when(f == 0)
    def _():
        x = x_ref[...]
        xb = x.astype(BF16)
        xb_ref[...] = xb
        o_ref[...] = (alpha / FFN_BETA) * x + swiglu_down(xb, *weights())

    @pl.when(jnp.logical_and(f > 0, f < last))
    def _():
        o_ref[...] += swiglu_down(xb_ref[...], *weights())

    @pl.when(f == last)
    def _():
        w = weights()
        rb = x_ref.shape[0] // row_blocks
        for r in range(row_blocks):
            rows = slice(r * rb, (r + 1) * rb)
            acc = o_ref[rows, :] + swiglu_down(xb_ref[rows, :], *w)
            o_ref[rows, :] = _layer_norm(acc, g_ref[...], b_ref[...], LN_EPS / (FFN_BETA * FFN_BETA))


def _ffn(x, wg, wu, wd, ln_g, ln_b, layer, alpha, tm=1024, tf=256, row_blocks=4, x_buffers=2, name="ffn"):
    m, d = x.shape
    ff = wg.shape[-1]
    wbytes = wg.dtype.itemsize
    assert ff // tf >= 2 and tm % (row_blocks * 2 * SUBLANES) == 0
    est = ((2 + x_buffers) * tm * d * 4 + 2 * 3 * d * tf * wbytes + tm * d * 2
           + 3 * d * tf * 2 + 2 * tm * tf * 4 + tm * tf * 2)
    return pl.pallas_call(
        functools.partial(_ffn_kernel, alpha=alpha, row_blocks=row_blocks),
        grid=(m // tm, ff // tf),
        in_specs=[
            pl.BlockSpec((tm, d), lambda i, f: (i, 0), pipeline_mode=pl.Buffered(x_buffers)),
            pl.BlockSpec((None, d, tf), lambda i, f: (layer, 0, f)),
            pl.BlockSpec((None, d, tf), lambda i, f: (layer, 0, f)),
            pl.BlockSpec((None, tf, d), lambda i, f: (layer, f, 0)),
            pl.BlockSpec((None, 1, d), lambda i, f: (layer, 0, 0)),
            pl.BlockSpec((None, 1, d), lambda i, f: (layer, 0, 0)),
        ],
        out_specs=pl.BlockSpec((tm, d), lambda i, f: (i, 0)),
        out_shape=jax.ShapeDtypeStruct((m, d), F32),
        scratch_shapes=[pltpu.VMEM((tm, d), BF16)],
        compiler_params=_params(("parallel", "arbitrary"), est),
        name=name,
    )(x, wg, wu, wd, ln_g, ln_b)


def _proj_kernel(x_ref, win_ref, cw_ref, cb_ref, gc_ref, u_ref, z_ref, us_ref, vpad_ref, *,
                 tiles_per_seq, row_blocks):
    tm = x_ref.shape[0]
    nt = u_ref.shape[0]
    d_s5 = nt * LANES
    d_conv = z_ref.shape[1]
    rb = tm // row_blocks
    cb = rb // S5_T

    @pl.when(pl.program_id(0) % tiles_per_seq == 0)
    def _():
        vpad_ref[0:SUBLANES, :] = jnp.zeros((SUBLANES, d_conv), F32)

    for r in range(row_blocks):
        r0 = r * rb
        xb = x_ref[r0:r0 + rb, :].astype(BF16)

        u = jnp.dot(xb, win_ref[:, 0:d_s5], preferred_element_type=F32)
        for j in range(nt):
            us_ref[j, r0:r0 + rb, :] = u[:, j * LANES:(j + 1) * LANES]
        for j in range(nt):
            for t in range(S5_T):
                rows = us_ref[j, pl.ds(r0 + t, cb, stride=S5_T), :]
                u_ref[j, r * cb:(r + 1) * cb, t * LANES:(t + 1) * LANES] = rows.astype(BF16)

        o = d_s5
        gate_c = jnp.dot(xb, win_ref[:, o + d_conv:o + 2 * d_conv], preferred_element_type=F32)
        hid = jnp.dot(xb, win_ref[:, o + 2 * d_conv:o + 3 * d_conv], preferred_element_type=F32)
        v = gate_c * hid
        p0 = SUBLANES + r0
        vpad_ref[p0:p0 + rb, :] = v
        v1 = vpad_ref[p0 - 1:p0 - 1 + rb, :]
        v2 = vpad_ref[p0 - 2:p0 - 2 + rb, :]
        conv = cb_ref[...] + cw_ref[0:1, :] * v2 + cw_ref[1:2, :] * v1 + cw_ref[2:3, :] * v

        gate_b = jnp.dot(xb, win_ref[:, o:o + d_conv], preferred_element_type=F32)
        z_ref[r0:r0 + rb, :] = _rms_norm(gate_b * conv, gc_ref[...]).astype(BF16)

    vpad_ref[0:SUBLANES, :] = vpad_ref[tm:tm + SUBLANES, :]


def _proj(x, w_in, conv_w, conv_b, g_conv, layer, seq, d_s5, tm=512, row_blocks=2, name="proj_conv"):
    m, d = x.shape
    d_in = w_in.shape[-1]
    d_conv = conv_b.shape[-1]
    nt = d_s5 // LANES
    assert CONV_W - 1 <= SUBLANES and conv_w.shape[1] == CONV_W and seq % tm == 0
    assert tm % (row_blocks * S5_T * 2 * SUBLANES) == 0
    rb = tm // row_blocks
    est = (2 * tm * d * 4 + d * d_in * 2 + 2 * tm * (d_s5 + d_conv) * 2
           + tm * d_s5 * 4 + (tm + SUBLANES) * d_conv * 4 + rb * d * 2 + 6 * rb * d_conv * 4)
    return pl.pallas_call(
        functools.partial(_proj_kernel, tiles_per_seq=seq // tm, row_blocks=row_blocks),
        grid=(m // tm,),
        in_specs=[
            pl.BlockSpec((tm, d), lambda i: (i, 0)),
            pl.BlockSpec((None, d, d_in), lambda i: (layer, 0, 0), pipeline_mode=pl.Buffered(1)),
            pl.BlockSpec((None, CONV_W, d_conv), lambda i: (layer, 0, 0)),
            pl.BlockSpec((None, 1, d_conv), lambda i: (layer, 0, 0)),
            pl.BlockSpec((None, 1, d_conv), lambda i: (layer, 0, 0)),
        ],
        out_specs=[
            pl.BlockSpec((nt, tm // S5_T, S5_ROW), lambda i: (0, i, 0)),
            pl.BlockSpec((tm, d_conv), lambda i: (i, 0)),
        ],
        out_shape=[
            jax.ShapeDtypeStruct((nt, m // S5_T, S5_ROW), BF16),
            jax.ShapeDtypeStruct((m, d_conv), BF16),
        ],
        scratch_shapes=[pltpu.VMEM((nt, tm, LANES), F32), pltpu.VMEM((tm + SUBLANES, d_conv), F32)],
        compiler_params=_params(("arbitrary",), est),
        name=name,
    )(x, w_in, conv_w, conv_b, g_conv)


def _discretize(lre, lim, ldt):
    dt = jnp.exp(ldt)
    mag = jnp.exp(lre * dt)
    ang = lim * dt
    return mag * jnp.cos(ang), mag * jnp.sin(ang)


def _s5_prep_kernel(bre_ref, bim_ref, cre_ref, cim_ref, lre_ref, lim_ref, ldt_ref,
                    slre_ref, slim_ref, sldt_ref, wt_ref, wl_ref, ws_ref, at_ref):
    lre, lim = lre_ref[...], lim_ref[...]
    ab_re, ab_im = _discretize(lre, lim, ldt_ref[...])
    den = lre * lre + lim * lim
    nr, ni = ab_re - 1.0, ab_im
    q_re = (nr * lre + ni * lim) / den
    q_im = (ni * lre - nr * lim) / den
    b_re, b_im = bre_ref[...], bim_ref[...]
    bb_re = q_re * b_re - q_im * b_im
    bb_im = q_re * b_im + q_im * b_re
    c_re, c_im = cre_ref[...], cim_ref[...]

    rows = lax.broadcasted_iota(jnp.int32, (S5_HALF, LANES), 0) // S5_N
    cols = lax.broadcasted_iota(jnp.int32, (S5_HALF, LANES), 1) // S5_P
    same_group = rows == cols
    same_group_sq = (lax.broadcasted_iota(jnp.int32, (LANES, LANES), 0) // S5_P
                     == lax.broadcasted_iota(jnp.int32, (LANES, LANES), 1) // S5_P)

    def blk(a):
        return jnp.where(same_group, jnp.concatenate([a] * S5_GPT, axis=0), 0.0)

    def dot_exact(a, b):
        return jnp.dot(a, b, precision=lax.Precision.HIGHEST, preferred_element_type=F32)

    p_re, p_im = jnp.ones_like(ab_re), jnp.zeros_like(ab_re)
    toeplitz = []
    for k in range(S5_T + 1):
        if k < S5_T:
            bl_re, bl_im = _cmul(p_re, p_im, bb_re, bb_im)
            tau = S5_T - 1 - k
            wl_ref[tau * LANES:(tau + 1) * LANES, 0:S5_HALF] = blk(bl_re).T.astype(BF16)
            wl_ref[tau * LANES:(tau + 1) * LANES, S5_HALF:S5_STATE] = blk(bl_im).T.astype(BF16)
            kk = dot_exact(bl_re.T, c_re) - dot_exact(bl_im.T, c_im)
            toeplitz.append(jnp.where(same_group_sq, kk, 0.0).astype(BF16))
        if k >= 1:
            cl_re, cl_im = _cmul(p_re, p_im, c_re, c_im)
            t = k - 1
            ws_ref[0:S5_HALF, t * LANES:(t + 1) * LANES] = blk(cl_re).astype(BF16)
            ws_ref[S5_HALF:S5_STATE, t * LANES:(t + 1) * LANES] = (-blk(cl_im)).astype(BF16)
        p_re, p_im = _cmul(p_re, p_im, ab_re, ab_im)

    zero = jnp.zeros((LANES, LANES), BF16)
    for tau in range(S5_T):
        for t in range(S5_T):
            wt_ref[tau * LANES:(tau + 1) * LANES, t * LANES:(t + 1) * LANES] = (
                toeplitz[t - tau] if t >= tau else zero)

    a_re, a_im = _discretize(slre_ref[...], slim_ref[...], sldt_ref[...])
    step = 1
    while step < S5_T:
        a_re, a_im = _cmul(a_re, a_im, a_re, a_im)
        step *= 2
    at_ref[0:1, :] = a_re
    at_ref[1:2, :] = a_im


def _s5_prep(lam_re, lam_im, log_dt, b_re, b_im, c_re, c_im):
    depth, g, n = lam_re.shape
    p = b_re.shape[-1]
    assert (p, n) == (S5_P, S5_N) and S5_T & (S5_T - 1) == 0
    nt = g // S5_GPT

    def b_layout(a):
        return a.reshape(depth, nt, S5_GPT, n, p).transpose(0, 1, 3, 2, 4).reshape(depth, nt, n, LANES)

    def c_layout(a):
        return a.reshape(depth, nt, S5_GPT, p, n).transpose(0, 1, 4, 2, 3).reshape(depth, nt, n, LANES)

    def lam_layout(a):
        a = a.reshape(depth, nt, S5_GPT, n).transpose(0, 1, 3, 2)
        return jnp.broadcast_to(a[..., None], (depth, nt, n, S5_GPT, p)).reshape(depth, nt, n, LANES)

    ldt = jnp.broadcast_to(log_dt.reshape(depth, nt, 1, S5_GPT, 1), (depth, nt, n, S5_GPT, p))
    ldt = ldt.reshape(depth, nt, n, LANES)
    s_lre = lam_re.reshape(depth, nt, 1, S5_HALF)
    s_lim = lam_im.reshape(depth, nt, 1, S5_HALF)
    s_ldt = jnp.broadcast_to(log_dt.reshape(depth, nt, S5_GPT, 1), (depth, nt, S5_GPT, n))
    s_ldt = s_ldt.reshape(depth, nt, 1, S5_HALF)

    tile_spec = pl.BlockSpec((None, None, n, LANES), lambda l, j: (l, j, 0, 0))
    lane_spec = pl.BlockSpec((None, None, 1, S5_HALF), lambda l, j: (l, j, 0, 0))
    est = 2 * 2 * (S5_ROW * S5_ROW + 2 * S5_ROW * S5_STATE)
    return pl.pallas_call(
        _s5_prep_kernel,
        grid=(depth, nt),
        in_specs=[tile_spec] * 7 + [lane_spec] * 3,
        out_specs=[
            pl.BlockSpec((None, None, S5_ROW, S5_ROW), lambda l, j: (l, j, 0, 0)),
            pl.BlockSpec((None, None, S5_ROW, S5_STATE), lambda l, j: (l, j, 0, 0)),
            pl.BlockSpec((None, None, S5_STATE, S5_ROW), lambda l, j: (l, j, 0, 0)),
            pl.BlockSpec((None, None, 2, S5_HALF), lambda l, j: (l, j, 0, 0)),
        ],
        out_shape=[
            jax.ShapeDtypeStruct((depth, nt, S5_ROW, S5_ROW), BF16),
            jax.ShapeDtypeStruct((depth, nt, S5_ROW, S5_STATE), BF16),
            jax.ShapeDtypeStruct((depth, nt, S5_STATE, S5_ROW), BF16),
            jax.ShapeDtypeStruct((depth, nt, 2, S5_HALF), F32),
        ],
        compiler_params=_params(("parallel", "parallel"), est),
        name="s5_prep",
    )(b_layout(b_re), b_layout(b_im), c_layout(c_re), c_layout(c_im),
      lam_layout(lam_re), lam_layout(lam_im), ldt, s_lre, s_lim, s_ldt)


def _s5_kernel(u_ref, wt_ref, wl_ref, ws_ref, at_ref, d_ref, y_ref, loc_ref, st_ref, *,
               batch, chunks, pitch, unroll):
    x = u_ref[...]
    loc = jnp.dot(x, wl_ref[...], preferred_element_type=F32)
    slabs = S5_STATE // LANES
    half = slabs // 2
    for k in range(slabs):
        for b in range(batch):
            loc_ref[k, b * pitch:b * pitch + chunks, :] = loc[b * chunks:(b + 1) * chunks, k * LANES:(k + 1) * LANES]
    a_re = [at_ref[0:1, k * LANES:(k + 1) * LANES] for k in range(half)]
    a_im = [at_ref[1:2, k * LANES:(k + 1) * LANES] for k in range(half)]

    def body(c, carry):
        rows = pl.ds(c, batch, stride=pitch)
        new = [None] * slabs
        for k in range(half):
            s_re, s_im = carry[k], carry[half + k]
            st_ref[k, rows, :] = s_re
            st_ref[half + k, rows, :] = s_im
            new[k] = a_re[k] * s_re - a_im[k] * s_im + loc_ref[k, rows, :]
            new[half + k] = a_re[k] * s_im + a_im[k] * s_re + loc_ref[half + k, rows, :]
        return tuple(new)

    zero = jnp.zeros((batch, LANES), F32)
    lax.fori_loop(0, chunks, body, (zero,) * slabs, unroll=unroll)

    y = jnp.dot(x, wt_ref[...], preferred_element_type=F32)
    st = jnp.concatenate(
        [jnp.concatenate([st_ref[k, b * pitch:b * pitch + chunks, :] for b in range(batch)], axis=0)
         for k in range(slabs)], axis=-1).astype(BF16)
    y = y + jnp.dot(st, ws_ref[...], preferred_element_type=F32)
    y_ref[...] = (y + d_ref[...] * x.astype(F32)).astype(BF16)


def _s5(u2, wt, wl, ws, at, d_rows, layer, batch, seq, unroll=2, name="s5_mix"):
    nt, rows, _ = u2.shape
    chunks = seq // S5_T
    assert rows == batch * chunks and chunks % SUBLANES == 0
    pitch = chunks + SUBLANES if (chunks // SUBLANES) % 2 == 0 else chunks
    est = (2 * 2 * rows * S5_ROW * 2 + 2 * 2 * (S5_ROW * S5_ROW + 2 * S5_ROW * S5_STATE)
           + 2 * batch * pitch * S5_STATE * 4 + 3 * rows * S5_ROW * 4 + rows * S5_STATE * 2)
    return pl.pallas_call(
        functools.partial(_s5_kernel, batch=batch, chunks=chunks, pitch=pitch, unroll=unroll),
        grid=(nt,),
        in_specs=[
            pl.BlockSpec((None, rows, S5_ROW), lambda j: (j, 0, 0)),
            pl.BlockSpec((None, None, S5_ROW, S5_ROW), lambda j: (layer, j, 0, 0)),
            pl.BlockSpec((None, None, S5_ROW, S5_STATE), lambda j: (layer, j, 0, 0)),
            pl.BlockSpec((None, None, S5_STATE, S5_ROW), lambda j: (layer, j, 0, 0)),
            pl.BlockSpec((None, None, 2, S5_HALF), lambda j: (layer, j, 0, 0)),
            pl.BlockSpec((None, None, 1, S5_ROW), lambda j: (layer, j, 0, 0)),
        ],
        out_specs=pl.BlockSpec((None, rows, S5_ROW), lambda j: (j, 0, 0)),
        out_shape=jax.ShapeDtypeStruct((nt, rows, S5_ROW), BF16),
        scratch_shapes=[pltpu.VMEM((S5_STATE // LANES, batch * pitch, LANES), F32)] * 2,
        compiler_params=_params(("parallel",), est),
        name=name,
    )(u2, wt, wl, ws, at, d_rows)


def _out_kernel(y_ref, z_ref, x_ref, wglu_ref, wout_ref, gs_ref, g_ref, b_ref, o_ref, ys_ref, *, alpha):
    nt = y_ref.shape[0]
    tm = x_ref.shape[0]
    for j in range(nt):
        for t in range(S5_T):
            ys_ref[j, pl.ds(t, tm // S5_T, stride=S5_T), :] = y_ref[j, :, t * LANES:(t + 1) * LANES].astype(F32)
    y = jnp.concatenate([ys_ref[j] for j in range(nt)], axis=-1)
    y = _gelu_tanh(y)
    y = y * _sigmoid(jnp.dot(y.astype(BF16), wglu_ref[...], preferred_element_type=F32))
    y = _rms_norm(y, gs_ref[...])
    yz = jnp.concatenate([y.astype(BF16), z_ref[...]], axis=-1)
    mix = jnp.dot(yz, wout_ref[...], preferred_element_type=F32)
    o_ref[...] = _layer_norm(alpha * x_ref[...] + mix, g_ref[...], b_ref[...])


def _out(y2, z, x, w_glu, w_out, g_s5, ln_g, ln_b, layer, alpha, tm=512, w_buffers=1, name="glu_out"):
    m, d = x.shape
    nt = y2.shape[0]
    d_s5 = nt * LANES
    d_conv = z.shape[1]
    assert tm % (S5_T * 2 * SUBLANES) == 0
    est = (2 * 2 * tm * d * 4 + 2 * tm * (d_s5 + d_conv) * 2
           + w_buffers * (d_s5 * d_s5 + (d_s5 + d_conv) * d) * 2
           + tm * d_s5 * 4 + 3 * tm * d_s5 * 4 + tm * d * 4)
    return pl.pallas_call(
        functools.partial(_out_kernel, alpha=alpha),
        grid=(m // tm,),
        in_specs=[
            pl.BlockSpec((nt, tm // S5_T, S5_ROW), lambda i: (0, i, 0)),
            pl.BlockSpec((tm, d_conv), lambda i: (i, 0)),
            pl.BlockSpec((tm, d), lambda i: (i, 0)),
            pl.BlockSpec((None, d_s5, d_s5), lambda i: (layer, 0, 0), pipeline_mode=pl.Buffered(w_buffers)),
            pl.BlockSpec((None, d_s5 + d_conv, d), lambda i: (layer, 0, 0),
                         pipeline_mode=pl.Buffered(w_buffers)),
            pl.BlockSpec((None, 1, d_s5), lambda i: (layer, 0, 0)),
            pl.BlockSpec((None, 1, d), lambda i: (layer, 0, 0)),
            pl.BlockSpec((None, 1, d), lambda i: (layer, 0, 0)),
        ],
        out_specs=pl.BlockSpec((tm, d), lambda i: (i, 0)),
        out_shape=jax.ShapeDtypeStruct((m, d), F32),
        scratch_shapes=[pltpu.VMEM((nt, tm, LANES), F32)],
        compiler_params=_params(("parallel",), est),
        name=name,
    )(y2, z, x, w_glu, w_out, g_s5, ln_g, ln_b)


def kernel(x, ffn1_gate, ffn1_up, ffn1_down, ln1_g, ln1_b, w_in, s5_lam_re, s5_lam_im, s5_log_dt,
           s5_b_re, s5_b_im, s5_c_re, s5_c_im, s5_d, s5_w_glu, conv_w, conv_b, g_s5, g_conv, w_out,
           ln2_g, ln2_b, ffn2_gate, ffn2_up, ffn2_down, ln3_g, ln3_b):
    batch, seq, d_model = x.shape
    depth = ffn1_gate.shape[0]
    d_s5 = s5_w_glu.shape[-1]
    alpha = (2.0 * depth) ** 0.25
    assert d_s5 % LANES == 0 and seq % S5_T == 0

    def row(a):
        return a.reshape(depth, 1, a.shape[-1])

    ffn1 = (ffn1_gate, ffn1_up, ffn1_down, row(ln1_g), row(ln1_b))
    ffn2 = (ffn2_gate, ffn2_up, ffn2_down, row(ln3_g), row(ln3_b))
    w_in_b, w_glu_b, w_out_b = w_in.astype(BF16), s5_w_glu.astype(BF16), w_out.astype(BF16)

    wt, wl, ws, at = _s5_prep(s5_lam_re, s5_lam_im, s5_log_dt, s5_b_re, s5_b_im, s5_c_re, s5_c_im)
    nt = d_s5 // LANES
    d_rows = jnp.tile(s5_d.reshape(depth, nt, 1, LANES), (1, 1, 1, S5_T))

    ffn_cfgs = [
        (1024, 256, 4, 2, False), (1024, 256, 1, 2, False),
        (512, 256, 2, 2, False), (512, 512, 2, 2, False),
        (1024, 256, 4, 1, False), (1024, 256, 4, 2, True),
        (1024, 512, 4, 2, True), (512, 512, 2, 2, True),
    ]
    proj_cfgs = [(512, 2), (256, 1), (512, 1), (512, 4)]
    out_cfgs = [(512, 1), (256, 1), (512, 2), (512, 1)]
    s5_unrolls = [2, 4, 8, 1]

    def ffn(hh, params, l, idx):
        tm, tf, rb, xbuf, precast = ffn_cfgs[idx % len(ffn_cfgs)]
        wg, wu, wd, g, b = params
        tag = f"ffn_c{idx}_tm{tm}_tf{tf}_rb{rb}_xb{xbuf}_{'bf16' if precast else 'f32'}"
        if precast:
            wg, wu, wd = (w[l:l + 1].astype(BF16) for w in (wg, wu, wd))
            g, b = g[l:l + 1], b[l:l + 1]
            return _ffn(hh, wg, wu, wd, g, b, 0, alpha, tm, tf, rb, xbuf, tag)
        return _ffn(hh, wg, wu, wd, g, b, l, alpha, tm, tf, rb, xbuf, tag)

    h = x.reshape(batch * seq, d_model)
    for l in range(depth):
        h = ffn(h, ffn1, l, 2 * l)
        ptm, prb = proj_cfgs[l % len(proj_cfgs)]
        u2, z = _proj(h, w_in_b, conv_w, row(conv_b), row(g_conv), l, seq, d_s5, ptm, prb,
                      f"proj_c{l}_tm{ptm}_rb{prb}")
        y2 = _s5(u2, wt, wl, ws, at, d_rows, l, batch, seq, s5_unrolls[l % len(s5_unrolls)],
                 f"s5mix_c{l}_unroll{s5_unrolls[l % len(s5_unrolls)]}")
        otm, owb = out_cfgs[l % len(out_cfgs)]
        h = _out(y2, z, h, w_glu_b, w_out_b, row(g_s5), row(ln2_g), row(ln2_b), l, alpha, otm, owb,
                 f"gluout_c{l}_tm{otm}_wb{owb}")
        h = ffn(h, ffn2, l, 2 * l + 1)
    return h.reshape(batch, seq, d_model)
```

```python
import functools

import jax
import jax.numpy as jnp
from jax import lax
from jax.experimental import pallas as pl
from jax.experimental.pallas import tpu as pltpu

F32 = jnp.float32
BF16 = jnp.bfloat16

LN_EPS = 1e-5
RMS_EPS = 1e-6
CONV_W = 3
FFN_BETA = 0.5

LANES = 128
SUBLANES = 8
V7X_VMEM_BYTES = 64 * 1024 * 1024
VMEM_CAP_BYTES = V7X_VMEM_BYTES - 2 * 1024 * 1024
COMPILER_TEMP_BYTES = 8 * 1024 * 1024

S5_P = 16
S5_N = 64
S5_T = 8
S5_GPT = LANES // S5_P
S5_HALF = S5_GPT * S5_N
S5_STATE = 2 * S5_HALF
S5_ROW = S5_T * LANES


def _layer_norm(y, g, b, eps=LN_EPS):
    mu = jnp.mean(y, axis=-1, keepdims=True)
    yc = y - mu
    var = jnp.mean(yc * yc, axis=-1, keepdims=True)
    return yc * lax.rsqrt(var + eps) * g + b


def _rms_norm(y, g):
    return y * lax.rsqrt(jnp.mean(y * y, axis=-1, keepdims=True) + RMS_EPS) * g


def _sigmoid(x):
    return 1.0 / (1.0 + jnp.exp(-x))


def _gelu_tanh(x):
    c = 0.7978845608028654
    return 0.5 * x * (1.0 + jnp.tanh(c * (x + 0.044715 * (x * x * x))))


def _cmul(ar, ai, br, bi):
    return ar * br - ai * bi, ar * bi + ai * br


def _params(sem, est_bytes):
    limit = min(VMEM_CAP_BYTES, est_bytes + COMPILER_TEMP_BYTES)
    return pltpu.CompilerParams(dimension_semantics=sem, vmem_limit_bytes=limit)


def _ffn_kernel(x_ref, wg_ref, wu_ref, wd_ref, g_ref, b_ref, *rest, alpha, cast_next):
    if cast_next:
        ng_ref, nu_ref, nd_ref, o_ref, cg_ref, cu_ref, cd_ref, xb_ref = rest
    else:
        o_ref, xb_ref = rest
    f = pl.program_id(1)
    last = pl.num_programs(1) - 1

    def weights():
        if cast_next:
            cg_ref[...] = ng_ref[...].astype(BF16)
            cu_ref[...] = nu_ref[...].astype(BF16)
            cd_ref[...] = nd_ref[...].astype(BF16)
        return wg_ref[...].astype(BF16), wu_ref[...].astype(BF16), wd_ref[...].astype(BF16)

    def swiglu_down(xb, wg, wu, wd):
        gate = jnp.dot(xb, wg, preferred_element_type=F32)
        up = jnp.dot(xb, wu, preferred_element_type=F32)
        h = (gate * _sigmoid(gate)) * up
        return jnp.dot(h.astype(BF16), wd, preferred_element_type=F32)

    @pl.when(f == 0)
    def _():
        x = x_ref[...]
        xb = x.astype(BF16)
        xb_ref[...] = xb
        o_ref[...] = (alpha / FFN_BETA) * x + swiglu_down(xb, *weights())

    @pl.when(jnp.logical_and(f > 0, f < last))
    def _():
        o_ref[...] += swiglu_down(xb_ref[...], *weights())

    @pl.when(f == last)
    def _():
        acc = o_ref[...] + swiglu_down(xb_ref[...], *weights())
        o_ref[...] = _layer_norm(acc, g_ref[...], b_ref[...], LN_EPS / (FFN_BETA * FFN_BETA))


def _ffn(x, weights, wlayer, ln_g, ln_b, layer, alpha, next_weights=None, next_layer=0, tm=1024):
    wg, wu, wd = weights
    m, d = x.shape
    ff = wg.shape[-1]
    wbytes = wg.dtype.itemsize
    tf = 2 * LANES * (4 // wbytes)
    ni, nf = m // tm, ff // tf
    cast_next = next_weights is not None
    dr = d // ni
    assert ff % tf == 0 and nf >= 2 and m % tm == 0 and d % (ni * LANES) == 0
    est = (2 * 2 * tm * d * 4 + 2 * 3 * d * tf * wbytes + tm * d * 2 + (wbytes // 4) * 3 * d * tf * 2
           + 2 * tm * tf * 4 + tm * tf * 2 + cast_next * 2 * 3 * dr * tf * 6)
    in_specs = [
        pl.BlockSpec((tm, d), lambda i, f: (i, 0)),
        pl.BlockSpec((None, d, tf), lambda i, f: (wlayer, 0, f)),
        pl.BlockSpec((None, d, tf), lambda i, f: (wlayer, 0, f)),
        pl.BlockSpec((None, tf, d), lambda i, f: (wlayer, f, 0)),
        pl.BlockSpec((None, 1, d), lambda i, f: (layer, 0, 0)),
        pl.BlockSpec((None, 1, d), lambda i, f: (layer, 0, 0)),
    ]
    out_specs = [pl.BlockSpec((tm, d), lambda i, f: (i, 0))]
    out_shape = [jax.ShapeDtypeStruct((m, d), F32)]
    args = [x, wg, wu, wd, ln_g, ln_b]
    if cast_next:
        in_specs += [
            pl.BlockSpec((None, dr, tf), lambda i, f: (next_layer, i, f)),
            pl.BlockSpec((None, dr, tf), lambda i, f: (next_layer, i, f)),
            pl.BlockSpec((None, tf, dr), lambda i, f: (next_layer, f, i)),
        ]
        out_specs += [
            pl.BlockSpec((None, dr, tf), lambda i, f: (0, i, f)),
            pl.BlockSpec((None, dr, tf), lambda i, f: (0, i, f)),
            pl.BlockSpec((None, tf, dr), lambda i, f: (0, f, i)),
        ]
        out_shape += [jax.ShapeDtypeStruct((1,) + w.shape[1:], BF16) for w in next_weights]
        args += list(next_weights)
    outs = pl.pallas_call(
        functools.partial(_ffn_kernel, alpha=alpha, cast_next=cast_next),
        grid=(ni, nf),
        in_specs=in_specs,
        out_specs=out_specs,
        out_shape=out_shape,
        scratch_shapes=[pltpu.VMEM((tm, d), BF16)],
        compiler_params=_params(("arbitrary", "arbitrary"), est),
        name="ffn_cast" if cast_next else "ffn",
    )(*args)
    return outs[0], tuple(outs[1:])


def _proj_kernel(x_ref, win_ref, cw_ref, cb_ref, gc_ref, u_ref, z_ref, us_ref, vpad_ref, *, tiles_per_seq):
    tm = x_ref.shape[0]
    nt = u_ref.shape[0]
    d_s5 = nt * LANES
    d_conv = z_ref.shape[1]

    @pl.when(pl.program_id(0) % tiles_per_seq == 0)
    def _():
        vpad_ref[0:SUBLANES, :] = jnp.zeros((SUBLANES, d_conv), F32)

    xb = x_ref[...].astype(BF16)

    u = jnp.dot(xb, win_ref[:, 0:d_s5], preferred_element_type=F32)
    for j in range(nt):
        us_ref[j] = u[:, j * LANES:(j + 1) * LANES]
    for j in range(nt):
        for t in range(S5_T):
            rows = us_ref[j, pl.ds(t, tm // S5_T, stride=S5_T), :]
            u_ref[j, :, t * LANES:(t + 1) * LANES] = rows.astype(BF16)

    o = d_s5
    gate_c = jnp.dot(xb, win_ref[:, o + d_conv:o + 2 * d_conv], preferred_element_type=F32)
    hid = jnp.dot(xb, win_ref[:, o + 2 * d_conv:o + 3 * d_conv], preferred_element_type=F32)
    v = gate_c * hid
    vpad_ref[SUBLANES:SUBLANES + tm, :] = v
    v1 = vpad_ref[SUBLANES - 1:SUBLANES - 1 + tm, :]
    v2 = vpad_ref[SUBLANES - 2:SUBLANES - 2 + tm, :]
    conv = cb_ref[...] + cw_ref[0:1, :] * v2 + cw_ref[1:2, :] * v1 + cw_ref[2:3, :] * v
    vpad_ref[0:SUBLANES, :] = vpad_ref[tm:tm + SUBLANES, :]

    gate_b = jnp.dot(xb, win_ref[:, o:o + d_conv], preferred_element_type=F32)
    z_ref[...] = _rms_norm(gate_b * conv, gc_ref[...]).astype(BF16)


def _proj(x, w_in, conv_w, conv_b, g_conv, layer, seq, d_s5, tm=512):
    m, d = x.shape
    d_in = w_in.shape[-1]
    d_conv = conv_b.shape[-1]
    nt = d_s5 // LANES
    assert CONV_W - 1 <= SUBLANES and conv_w.shape[1] == CONV_W and seq % tm == 0
    assert tm % (S5_T * 2 * SUBLANES) == 0
    est = (2 * tm * d * 4 + d * d_in * 2 + 2 * tm * (d_s5 + d_conv) * 2
           + tm * d_s5 * 4 + (tm + SUBLANES) * d_conv * 4 + tm * d * 2 + 6 * tm * d_conv * 4)
    return pl.pallas_call(
        functools.partial(_proj_kernel, tiles_per_seq=seq // tm),
        grid=(m // tm,),
        in_specs=[
            pl.BlockSpec((tm, d), lambda i: (i, 0)),
            pl.BlockSpec((None, d, d_in), lambda i: (layer, 0, 0), pipeline_mode=pl.Buffered(1)),
            pl.BlockSpec((None, CONV_W, d_conv), lambda i: (layer, 0, 0)),
            pl.BlockSpec((None, 1, d_conv), lambda i: (layer, 0, 0)),
            pl.BlockSpec((None, 1, d_conv), lambda i: (layer, 0, 0)),
        ],
        out_specs=[
            pl.BlockSpec((nt, tm // S5_T, S5_ROW), lambda i: (0, i, 0)),
            pl.BlockSpec((tm, d_conv), lambda i: (i, 0)),
        ],
        out_shape=[
            jax.ShapeDtypeStruct((nt, m // S5_T, S5_ROW), BF16),
            jax.ShapeDtypeStruct((m, d_conv), BF16),
        ],
        scratch_shapes=[pltpu.VMEM((nt, tm, LANES), F32), pltpu.VMEM((tm + SUBLANES, d_conv), F32)],
        compiler_params=_params(("arbitrary",), est),
        name="proj_conv",
    )(x, w_in, conv_w, conv_b, g_conv)


def _discretize(lre, lim, ldt):
    dt = jnp.exp(ldt)
    mag = jnp.exp(lre * dt)
    ang = lim * dt
    return mag * jnp.cos(ang), mag * jnp.sin(ang)


def _s5_prep_kernel(bre_ref, bim_ref, cre_ref, cim_ref, lre_ref, lim_ref, ldt_ref,
                    slre_ref, slim_ref, sldt_ref, wt_ref, wl_ref, ws_ref, at_ref):
    lre, lim = lre_ref[...], lim_ref[...]
    ab_re, ab_im = _discretize(lre, lim, ldt_ref[...])
    den = lre * lre + lim * lim
    nr, ni = ab_re - 1.0, ab_im
    q_re = (nr * lre + ni * lim) / den
    q_im = (ni * lre - nr * lim) / den
    b_re, b_im = bre_ref[...], bim_ref[...]
    bb_re = q_re * b_re - q_im * b_im
    bb_im = q_re * b_im + q_im * b_re
    c_re, c_im = cre_ref[...], cim_ref[...]

    rows = lax.broadcasted_iota(jnp.int32, (S5_HALF, LANES), 0) // S5_N
    cols = lax.broadcasted_iota(jnp.int32, (S5_HALF, LANES), 1) // S5_P
    same_group = rows == cols
    same_group_sq = (lax.broadcasted_iota(jnp.int32, (LANES, LANES), 0) // S5_P
                     == lax.broadcasted_iota(jnp.int32, (LANES, LANES), 1) // S5_P)

    def blk(a):
        return jnp.where(same_group, jnp.concatenate([a] * S5_GPT, axis=0), 0.0)

    def dot_exact(a, b):
        return jnp.dot(a, b, precision=lax.Precision.HIGHEST, preferred_element_type=F32)

    p_re, p_im = jnp.ones_like(ab_re), jnp.zeros_like(ab_re)
    toeplitz = []
    for k in range(S5_T + 1):
        if k < S5_T:
            bl_re, bl_im = _cmul(p_re, p_im, bb_re, bb_im)
            tau = S5_T - 1 - k
            wl_ref[tau * LANES:(tau + 1) * LANES, 0:S5_HALF] = blk(bl_re).T.astype(BF16)
            wl_ref[tau * LANES:(tau + 1) * LANES, S5_HALF:S5_STATE] = blk(bl_im).T.astype(BF16)
            kk = dot_exact(bl_re.T, c_re) - dot_exact(bl_im.T, c_im)
            toeplitz.append(jnp.where(same_group_sq, kk, 0.0).astype(BF16))
        if k >= 1:
            cl_re, cl_im = _cmul(p_re, p_im, c_re, c_im)
            t = k - 1
            ws_ref[0:S5_HALF, t * LANES:(t + 1) * LANES] = blk(cl_re).astype(BF16)
            ws_ref[S5_HALF:S5_STATE, t * LANES:(t + 1) * LANES] = (-blk(cl_im)).astype(BF16)
        p_re, p_im = _cmul(p_re, p_im, ab_re, ab_im)

    zero = jnp.zeros((LANES, LANES), BF16)
    for tau in range(S5_T):
        for t in range(S5_T):
            wt_ref[tau * LANES:(tau + 1) * LANES, t * LANES:(t + 1) * LANES] = (
                toeplitz[t - tau] if t >= tau else zero)

    a_re, a_im = _discretize(slre_ref[...], slim_ref[...], sldt_ref[...])
    step = 1
    while step < S5_T:
        a_re, a_im = _cmul(a_re, a_im, a_re, a_im)
        step *= 2
    at_ref[0:1, :] = a_re
    at_ref[1:2, :] = a_im


def _s5_prep(lam_re, lam_im, log_dt, b_re, b_im, c_re, c_im):
    depth, g, n = lam_re.shape
    p = b_re.shape[-1]
    assert (p, n) == (S5_P, S5_N) and S5_T & (S5_T - 1) == 0
    nt = g // S5_GPT

    def b_layout(a):
        return a.reshape(depth, nt, S5_GPT, n, p).transpose(0, 1, 3, 2, 4).reshape(depth, nt, n, LANES)

    def c_layout(a):
        return a.reshape(depth, nt, S5_GPT, p, n).transpose(0, 1, 4, 2, 3).reshape(depth, nt, n, LANES)

    def lam_layout(a):
        a = a.reshape(depth, nt, S5_GPT, n).transpose(0, 1, 3, 2)
        return jnp.broadcast_to(a[..., None], (depth, nt, n, S5_GPT, p)).reshape(depth, nt, n, LANES)

    ldt = jnp.broadcast_to(log_dt.reshape(depth, nt, 1, S5_GPT, 1), (depth, nt, n, S5_GPT, p))
    ldt = ldt.reshape(depth, nt, n, LANES)
    s_lre = lam_re.reshape(depth, nt, 1, S5_HALF)
    s_lim = lam_im.reshape(depth, nt, 1, S5_HALF)
    s_ldt = jnp.broadcast_to(log_dt.reshape(depth, nt, S5_GPT, 1), (depth, nt, S5_GPT, n))
    s_ldt = s_ldt.reshape(depth, nt, 1, S5_HALF)

    tile_spec = pl.BlockSpec((None, None, n, LANES), lambda l, j: (l, j, 0, 0))
    lane_spec = pl.BlockSpec((None, None, 1, S5_HALF), lambda l, j: (l, j, 0, 0))
    est = 2 * 2 * (S5_ROW * S5_ROW + 2 * S5_ROW * S5_STATE)
    return pl.pallas_call(
        _s5_prep_kernel,
        grid=(depth, nt),
        in_specs=[tile_spec] * 7 + [lane_spec] * 3,
        out_specs=[
            pl.BlockSpec((None, None, S5_ROW, S5_ROW), lambda l, j: (l, j, 0, 0)),
            pl.BlockSpec((None, None, S5_ROW, S5_STATE), lambda l, j: (l, j, 0, 0)),
            pl.BlockSpec((None, None, S5_STATE, S5_ROW), lambda l, j: (l, j, 0, 0)),
            pl.BlockSpec((None, None, 2, S5_HALF), lambda l, j: (l, j, 0, 0)),
        ],
        out_shape=[
            jax.ShapeDtypeStruct((depth, nt, S5_ROW, S5_ROW), BF16),
            jax.ShapeDtypeStruct((depth, nt, S5_ROW, S5_STATE), BF16),
            jax.ShapeDtypeStruct((depth, nt, S5_STATE, S5_ROW), BF16),
            jax.ShapeDtypeStruct((depth, nt, 2, S5_HALF), F32),
        ],
        compiler_params=_params(("parallel", "parallel"), est),
        name="s5_prep",
    )(b_layout(b_re), b_layout(b_im), c_layout(c_re), c_layout(c_im),
      lam_layout(lam_re), lam_layout(lam_im), ldt, s_lre, s_lim, s_ldt)


def _s5_kernel(u_ref, wt_ref, wl_ref, ws_ref, at_ref, d_ref, y_ref, loc_ref, st_ref, *, batch, chunks, pitch):
    x = u_ref[...]
    loc = jnp.dot(x, wl_ref[...], preferred_element_type=F32)
    slabs = S5_STATE // LANES
    half = slabs // 2
    for k in range(slabs):
        for b in range(batch):
            loc_ref[k, b * pitch:b * pitch + chunks, :] = loc[b * chunks:(b + 1) * chunks, k * LANES:(k + 1) * LANES]
    a_re = [at_ref[0:1, k * LANES:(k + 1) * LANES] for k in range(half)]
    a_im = [at_ref[1:2, k * LANES:(k + 1) * LANES] for k in range(half)]

    def body(c, carry):
        rows = pl.ds(c, batch, stride=pitch)
        new = [None] * slabs
        for k in range(half):
            s_re, s_im = carry[k], carry[half + k]
            st_ref[k, rows, :] = s_re
            st_ref[half + k, rows, :] = s_im
            new[k] = a_re[k] * s_re - a_im[k] * s_im + loc_ref[k, rows, :]
            new[half + k] = a_re[k] * s_im + a_im[k] * s_re + loc_ref[half + k, rows, :]
        return tuple(new)

    zero = jnp.zeros((batch, LANES), F32)
    lax.fori_loop(0, chunks, body, (zero,) * slabs, unroll=SUBLANES)

    y = jnp.dot(x, wt_ref[...], preferred_element_type=F32)
    st = jnp.concatenate(
        [jnp.concatenate([st_ref[k, b * pitch:b * pitch + chunks, :] for b in range(batch)], axis=0)
         for k in range(slabs)], axis=-1).astype(BF16)
    y = y + jnp.dot(st, ws_ref[...], preferred_element_type=F32)
    y_ref[...] = (y + d_ref[...] * x.astype(F32)).astype(BF16)


def _s5(u2, wt, wl, ws, at, d_rows, layer, batch, seq):
    nt, rows, _ = u2.shape
    chunks = seq // S5_T
    assert rows == batch * chunks and chunks % SUBLANES == 0
    pitch = chunks + SUBLANES if (chunks // SUBLANES) % 2 == 0 else chunks
    est = (2 * 2 * rows * S5_ROW * 2 + 2 * 2 * (S5_ROW * S5_ROW + 2 * S5_ROW * S5_STATE)
           + 2 * batch * pitch * S5_STATE * 4 + 3 * rows * S5_ROW * 4 + rows * S5_STATE * 2)
    return pl.pallas_call(
        functools.partial(_s5_kernel, batch=batch, chunks=chunks, pitch=pitch),
        grid=(nt,),
        in_specs=[
            pl.BlockSpec((None, rows, S5_ROW), lambda j: (j, 0, 0)),
            pl.BlockSpec((None, None, S5_ROW, S5_ROW), lambda j: (layer, j, 0, 0)),
            pl.BlockSpec((None, None, S5_ROW, S5_STATE), lambda j: (layer, j, 0, 0)),
            pl.BlockSpec((None, None, S5_STATE, S5_ROW), lambda j: (layer, j, 0, 0)),
            pl.BlockSpec((None, None, 2, S5_HALF), lambda j: (layer, j, 0, 0)),
            pl.BlockSpec((None, None, 1, S5_ROW), lambda j: (layer, j, 0, 0)),
        ],
        out_specs=pl.BlockSpec((None, rows, S5_ROW), lambda j: (j, 0, 0)),
        out_shape=jax.ShapeDtypeStruct((nt, rows, S5_ROW), BF16),
        scratch_shapes=[pltpu.VMEM((S5_STATE // LANES, batch * pitch, LANES), F32)] * 2,
        compiler_params=_params(("parallel",), est),
        name="s5_mix",
    )(u2, wt, wl, ws, at, d_rows)


def _out_kernel(y_ref, z_ref, x_ref, wglu_ref, wout_ref, gs_ref, g_ref, b_ref, o_ref, ys_ref, *, alpha):
    nt = y_ref.shape[0]
    tm = x_ref.shape[0]
    for j in range(nt):
        for t in range(S5_T):
            ys_ref[j, pl.ds(t, tm // S5_T, stride=S5_T), :] = y_ref[j, :, t * LANES:(t + 1) * LANES].astype(F32)
    y = jnp.concatenate([ys_ref[j] for j in range(nt)], axis=-1)
    y = _gelu_tanh(y)
    y = y * _sigmoid(jnp.dot(y.astype(BF16), wglu_ref[...], preferred_element_type=F32))
    y = _rms_norm(y, gs_ref[...])
    yz = jnp.concatenate([y.astype(BF16), z_ref[...]], axis=-1)
    mix = jnp.dot(yz, wout_ref[...], preferred_element_type=F32)
    o_ref[...] = _layer_norm(alpha * x_ref[...] + mix, g_ref[...], b_ref[...])


def _out(y2, z, x, w_glu, w_out, g_s5, ln_g, ln_b, layer, alpha, tm=512):
    m, d = x.shape
    nt = y2.shape[0]
    d_s5 = nt * LANES
    d_conv = z.shape[1]
    assert tm % (S5_T * 2 * SUBLANES) == 0
    est = (2 * 2 * tm * d * 4 + 2 * tm * (d_s5 + d_conv) * 2
           + (d_s5 * d_s5 + (d_s5 + d_conv) * d) * 2
           + tm * d_s5 * 4 + 3 * tm * d_s5 * 4 + tm * d * 4)
    return pl.pallas_call(
        functools.partial(_out_kernel, alpha=alpha),
        grid=(m // tm,),
        in_specs=[
            pl.BlockSpec((nt, tm // S5_T, S5_ROW), lambda i: (0, i, 0)),
            pl.BlockSpec((tm, d_conv), lambda i: (i, 0)),
            pl.BlockSpec((tm, d), lambda i: (i, 0)),
            pl.BlockSpec((None, d_s5, d_s5), lambda i: (layer, 0, 0), pipeline_mode=pl.Buffered(1)),
            pl.BlockSpec((None, d_s5 + d_conv, d), lambda i: (layer, 0, 0), pipeline_mode=pl.Buffered(1)),
            pl.BlockSpec((None, 1, d_s5), lambda i: (layer, 0, 0)),
            pl.BlockSpec((None, 1, d), lambda i: (layer, 0, 0)),
            pl.BlockSpec((None, 1, d), lambda i: (layer, 0, 0)),
        ],
        out_specs=pl.BlockSpec((tm, d), lambda i: (i, 0)),
        out_shape=jax.ShapeDtypeStruct((m, d), F32),
        scratch_shapes=[pltpu.VMEM((nt, tm, LANES), F32)],
        compiler_params=_params(("parallel",), est),
        name="glu_out",
    )(y2, z, x, w_glu, w_out, g_s5, ln_g, ln_b)


def kernel(x, ffn1_gate, ffn1_up, ffn1_down, ln1_g, ln1_b, w_in, s5_lam_re, s5_lam_im, s5_log_dt,
           s5_b_re, s5_b_im, s5_c_re, s5_c_im, s5_d, s5_w_glu, conv_w, conv_b, g_s5, g_conv, w_out,
           ln2_g, ln2_b, ffn2_gate, ffn2_up, ffn2_down, ln3_g, ln3_b):
    batch, seq, d_model = x.shape
    depth = ffn1_gate.shape[0]
    d_s5 = s5_w_glu.shape[-1]
    alpha = (2.0 * depth) ** 0.25
    assert d_s5 % LANES == 0 and seq % S5_T == 0

    def row(a):
        return a.reshape(depth, 1, a.shape[-1])

    ffns = []
    for l in range(depth):
        ffns.append(((ffn1_gate, ffn1_up, ffn1_down), row(ln1_g), row(ln1_b), l))
        ffns.append(((ffn2_gate, ffn2_up, ffn2_down), row(ln3_g), row(ln3_b), l))
    w_in_b, w_glu_b, w_out_b = w_in.astype(BF16), s5_w_glu.astype(BF16), w_out.astype(BF16)

    wt, wl, ws, at = _s5_prep(s5_lam_re, s5_lam_im, s5_log_dt, s5_b_re, s5_b_im, s5_c_re, s5_c_im)
    nt = d_s5 // LANES
    d_rows = jnp.tile(s5_d.reshape(depth, nt, 1, LANES), (1, 1, 1, S5_T))

    state = {"h": x.reshape(batch * seq, d_model), "weights": ffns[0][0], "wlayer": 0, "k": 0}

    def ffn():
        k = state["k"]
        _, g, b, l = ffns[k]
        nxt = ffns[k + 1] if k + 1 < len(ffns) else None
        state["h"], cast = _ffn(state["h"], state["weights"], state["wlayer"], g, b, l, alpha,
                                nxt[0] if nxt else None, nxt[3] if nxt else 0)
        state["weights"], state["wlayer"], state["k"] = cast, 0, k + 1

    for l in range(depth):
        ffn()
        h = state["h"]
        u2, z = _proj(h, w_in_b, conv_w, row(conv_b), row(g_conv), l, seq, d_s5)
        y2 = _s5(u2, wt, wl, ws, at, d_rows, l, batch, seq)
        state["h"] = _out(y2, z, h, w_glu_b, w_out_b, row(g_s5), row(ln2_g), row(ln2_b), l, alpha)
        ffn()
    return state["h"].reshape(batch, seq, d_model)
```

```python
import functools

import jax
import jax.numpy as jnp
from jax import lax
from jax.experimental import pallas as pl
from jax.experimental.pallas import tpu as pltpu

F32 = jnp.float32
BF16 = jnp.bfloat16

LN_EPS = 1e-5
RMS_EPS = 1e-6
CONV_W = 3
FFN_BETA = 0.5

LANES = 128
SUBLANES = 8
V7X_VMEM_BYTES = 64 * 1024 * 1024
VMEM_CAP_BYTES = V7X_VMEM_BYTES - 2 * 1024 * 1024
COMPILER_TEMP_BYTES = 8 * 1024 * 1024

S5_P = 16
S5_N = 64
S5_T = 8
S5_GPT = LANES // S5_P
S5_HALF = S5_GPT * S5_N
S5_STATE = 2 * S5_HALF
S5_ROW = S5_T * LANES


def _layer_norm(y, g, b, eps=LN_EPS):
    mu = jnp.mean(y, axis=-1, keepdims=True)
    yc = y - mu
    var = jnp.mean(yc * yc, axis=-1, keepdims=True)
    return yc * lax.rsqrt(var + eps) * g + b


def _rms_norm(y, g):
    return y * lax.rsqrt(jnp.mean(y * y, axis=-1, keepdims=True) + RMS_EPS) * g


def _sigmoid(x):
    return 1.0 / (1.0 + jnp.exp(-x))


def _gelu_tanh(x):
    c = 0.7978845608028654
    return 0.5 * x * (1.0 + jnp.tanh(c * (x + 0.044715 * (x * x * x))))


def _cmul(ar, ai, br, bi):
    return ar * br - ai * bi, ar * bi + ai * br


def _params(sem, est_bytes):
    limit = min(VMEM_CAP_BYTES, est_bytes + COMPILER_TEMP_BYTES)
    return pltpu.CompilerParams(dimension_semantics=sem, vmem_limit_bytes=limit)


def _ffn_kernel(x_ref, wg_ref, wu_ref, wd_ref, g_ref, b_ref, *rest, alpha, cast_next):
    if cast_next:
        ng_ref, nu_ref, nd_ref, o_ref, cg_ref, cu_ref, cd_ref, xb_ref = rest
    else:
        o_ref, xb_ref = rest
    f = pl.program_id(1)
    last = pl.num_programs(1) - 1

    def weights():
        if cast_next:
            cg_ref[...] = ng_ref[...].astype(BF16)
            cu_ref[...] = nu_ref[...].astype(BF16)
            cd_ref[...] = nd_ref[...].astype(BF16)
        return wg_ref[...].astype(BF16), wu_ref[...].astype(BF16), wd_ref[...].astype(BF16)

    def swiglu_down(xb, wg, wu, wd):
        gate = jnp.dot(xb, wg, preferred_element_type=F32)
        up = jnp.dot(xb, wu, preferred_element_type=F32)
        h = (gate * _sigmoid(gate)) * up
        return jnp.dot(h.astype(BF16), wd, preferred_element_type=F32)

    @pl.when(f == 0)
    def _():
        x = x_ref[...]
        xb = x.astype(BF16)
        xb_ref[...] = xb
        o_ref[...] = (alpha / FFN_BETA) * x + swiglu_down(xb, *weights())

    @pl.when(jnp.logical_and(f > 0, f < last))
    def _():
        o_ref[...] += swiglu_down(xb_ref[...], *weights())

    @pl.when(f == last)
    def _():
        acc = o_ref[...] + swiglu_down(xb_ref[...], *weights())
        o_ref[...] = _layer_norm(acc, g_ref[...], b_ref[...], LN_EPS / (FFN_BETA * FFN_BETA))


def _ffn(x, weights, wlayer, ln_g, ln_b, layer, alpha, next_weights=None, next_layer=0, tm=1024):
    wg, wu, wd = weights
    m, d = x.shape
    ff = wg.shape[-1]
    wbytes = wg.dtype.itemsize
    tf = 2 * LANES * (4 // wbytes)
    ni, nf = m // tm, ff // tf
    cast_next = next_weights is not None
    dr = d // ni
    assert ff % tf == 0 and nf >= 2 and m % tm == 0 and d % (ni * LANES) == 0
    est = (2 * 2 * tm * d * 4 + 2 * 3 * d * tf * wbytes + tm * d * 2 + (wbytes // 4) * 3 * d * tf * 2
           + 2 * tm * tf * 4 + tm * tf * 2 + cast_next * 2 * 3 * dr * tf * 6)
    in_specs = [
        pl.BlockSpec((tm, d), lambda i, f: (i, 0)),
        pl.BlockSpec((None, d, tf), lambda i, f: (wlayer, 0, f)),
        pl.BlockSpec((None, d, tf), lambda i, f: (wlayer, 0, f)),
        pl.BlockSpec((None, tf, d), lambda i, f: (wlayer, f, 0)),
        pl.BlockSpec((None, 1, d), lambda i, f: (layer, 0, 0)),
        pl.BlockSpec((None, 1, d), lambda i, f: (layer, 0, 0)),
    ]
    out_specs = [pl.BlockSpec((tm, d), lambda i, f: (i, 0))]
    out_shape = [jax.ShapeDtypeStruct((m, d), F32)]
    args = [x, wg, wu, wd, ln_g, ln_b]
    if cast_next:
        in_specs += [
            pl.BlockSpec((None, dr, tf), lambda i, f: (next_layer, i, f)),
            pl.BlockSpec((None, dr, tf), lambda i, f: (next_layer, i, f)),
            pl.BlockSpec((None, tf, dr), lambda i, f: (next_layer, f, i)),
        ]
        out_specs += [
            pl.BlockSpec((None, dr, tf), lambda i, f: (0, i, f)),
            pl.BlockSpec((None, dr, tf), lambda i, f: (0, i, f)),
            pl.BlockSpec((None, tf, dr), lambda i, f: (0, f, i)),
        ]
        out_shape += [jax.ShapeDtypeStruct((1,) + w.shape[1:], BF16) for w in next_weights]
        args += list(next_weights)
    outs = pl.pallas_call(
        functools.partial(_ffn_kernel, alpha=alpha, cast_next=cast_next),
        grid=(ni, nf),
        in_specs=in_specs,
        out_specs=out_specs,
        out_shape=out_shape,
        scratch_shapes=[pltpu.VMEM((tm, d), BF16)],
        compiler_params=_params(("arbitrary", "arbitrary"), est),
        name="ffn_cast" if cast_next else "ffn",
    )(*args)
    return outs[0], tuple(outs[1:])


def _proj_kernel(x_ref, win_ref, cw_ref, cb_ref, gc_ref, nglu_ref, nout_ref,
                 u_ref, z_ref, cglu_ref, cout_ref, us_ref, vpad_ref, *, tiles_per_seq):
    tm = x_ref.shape[0]
    nt = u_ref.shape[0]
    d_s5 = nt * LANES
    d_conv = z_ref.shape[1]

    @pl.when(pl.program_id(0) % tiles_per_seq == 0)
    def _():
        vpad_ref[0:SUBLANES, :] = jnp.zeros((SUBLANES, d_conv), F32)

    cglu_ref[...] = nglu_ref[...].astype(BF16)
    cout_ref[...] = nout_ref[...].astype(BF16)

    xb = x_ref[...].astype(BF16)

    u = jnp.dot(xb, win_ref[:, 0:d_s5], preferred_element_type=F32)
    for j in range(nt):
        us_ref[j] = u[:, j * LANES:(j + 1) * LANES]
    for j in range(nt):
        for t in range(S5_T):
            rows = us_ref[j, pl.ds(t, tm // S5_T, stride=S5_T), :]
            u_ref[j, :, t * LANES:(t + 1) * LANES] = rows.astype(BF16)

    o = d_s5
    gate_c = jnp.dot(xb, win_ref[:, o + d_conv:o + 2 * d_conv], preferred_element_type=F32)
    hid = jnp.dot(xb, win_ref[:, o + 2 * d_conv:o + 3 * d_conv], preferred_element_type=F32)
    v = gate_c * hid
    vpad_ref[SUBLANES:SUBLANES + tm, :] = v
    v1 = vpad_ref[SUBLANES - 1:SUBLANES - 1 + tm, :]
    v2 = vpad_ref[SUBLANES - 2:SUBLANES - 2 + tm, :]
    conv = cb_ref[...] + cw_ref[0:1, :] * v2 + cw_ref[1:2, :] * v1 + cw_ref[2:3, :] * v
    vpad_ref[0:SUBLANES, :] = vpad_ref[tm:tm + SUBLANES, :]

    gate_b = jnp.dot(xb, win_ref[:, o:o + d_conv], preferred_element_type=F32)
    z_ref[...] = _rms_norm(gate_b * conv, gc_ref[...]).astype(BF16)


def _proj(x, w_in, conv_w, conv_b, g_conv, w_glu, w_out, layer, seq, d_s5, tm=512):
    m, d = x.shape
    d_in = w_in.shape[-1]
    d_conv = conv_b.shape[-1]
    nt = d_s5 // LANES
    steps = m // tm
    rg, ro = w_glu.shape[1] // steps, w_out.shape[1] // steps
    assert CONV_W - 1 <= SUBLANES and conv_w.shape[1] == CONV_W and seq % tm == 0
    assert tm % (S5_T * 2 * SUBLANES) == 0
    assert w_glu.shape[1] % (steps * 2 * SUBLANES) == 0 and w_out.shape[1] % (steps * 2 * SUBLANES) == 0
    est = (2 * tm * d * 4 + d * d_in * 2 + 2 * tm * (d_s5 + d_conv) * 2
           + tm * d_s5 * 4 + (tm + SUBLANES) * d_conv * 4 + tm * d * 2 + 6 * tm * d_conv * 4
           + 2 * (rg * w_glu.shape[2] + ro * w_out.shape[2]) * 6)
    return pl.pallas_call(
        functools.partial(_proj_kernel, tiles_per_seq=seq // tm),
        grid=(steps,),
        in_specs=[
            pl.BlockSpec((tm, d), lambda i: (i, 0)),
            pl.BlockSpec((None, d, d_in), lambda i: (0, 0, 0), pipeline_mode=pl.Buffered(1)),
            pl.BlockSpec((None, CONV_W, d_conv), lambda i: (layer, 0, 0)),
            pl.BlockSpec((None, 1, d_conv), lambda i: (layer, 0, 0)),
            pl.BlockSpec((None, 1, d_conv), lambda i: (layer, 0, 0)),
            pl.BlockSpec((None, rg, w_glu.shape[2]), lambda i: (layer, i, 0)),
            pl.BlockSpec((None, ro, w_out.shape[2]), lambda i: (layer, i, 0)),
        ],
        out_specs=[
            pl.BlockSpec((nt, tm // S5_T, S5_ROW), lambda i: (0, i, 0)),
            pl.BlockSpec((tm, d_conv), lambda i: (i, 0)),
            pl.BlockSpec((None, rg, w_glu.shape[2]), lambda i: (0, i, 0)),
            pl.BlockSpec((None, ro, w_out.shape[2]), lambda i: (0, i, 0)),
        ],
        out_shape=[
            jax.ShapeDtypeStruct((nt, m // S5_T, S5_ROW), BF16),
            jax.ShapeDtypeStruct((m, d_conv), BF16),
            jax.ShapeDtypeStruct((1,) + w_glu.shape[1:], BF16),
            jax.ShapeDtypeStruct((1,) + w_out.shape[1:], BF16),
        ],
        scratch_shapes=[pltpu.VMEM((nt, tm, LANES), F32), pltpu.VMEM((tm + SUBLANES, d_conv), F32)],
        compiler_params=_params(("arbitrary",), est),
        name="proj_conv",
    )(x, w_in, conv_w, conv_b, g_conv, w_glu, w_out)


def _discretize(lre, lim, ldt):
    dt = jnp.exp(ldt)
    mag = jnp.exp(lre * dt)
    ang = lim * dt
    return mag * jnp.cos(ang), mag * jnp.sin(ang)


def _s5_prep_kernel(bre_ref, bim_ref, cre_ref, cim_ref, lre_ref, lim_ref, ldt_ref,
                    slre_ref, slim_ref, sldt_ref, wt_ref, wl_ref, ws_ref, at_ref):
    lre, lim = lre_ref[...], lim_ref[...]
    ab_re, ab_im = _discretize(lre, lim, ldt_ref[...])
    den = lre * lre + lim * lim
    nr, ni = ab_re - 1.0, ab_im
    q_re = (nr * lre + ni * lim) / den
    q_im = (ni * lre - nr * lim) / den
    b_re, b_im = bre_ref[...], bim_ref[...]
    bb_re = q_re * b_re - q_im * b_im
    bb_im = q_re * b_im + q_im * b_re
    c_re, c_im = cre_ref[...], cim_ref[...]

    rows = lax.broadcasted_iota(jnp.int32, (S5_HALF, LANES), 0) // S5_N
    cols = lax.broadcasted_iota(jnp.int32, (S5_HALF, LANES), 1) // S5_P
    same_group = rows == cols
    same_group_sq = (lax.broadcasted_iota(jnp.int32, (LANES, LANES), 0) // S5_P
                     == lax.broadcasted_iota(jnp.int32, (LANES, LANES), 1) // S5_P)

    def blk(a):
        return jnp.where(same_group, jnp.concatenate([a] * S5_GPT, axis=0), 0.0)

    def dot_exact(a, b):
        return jnp.dot(a, b, precision=lax.Precision.HIGHEST, preferred_element_type=F32)

    p_re, p_im = jnp.ones_like(ab_re), jnp.zeros_like(ab_re)
    toeplitz = []
    for k in range(S5_T + 1):
        if k < S5_T:
            bl_re, bl_im = _cmul(p_re, p_im, bb_re, bb_im)
            tau = S5_T - 1 - k
            wl_ref[tau * LANES:(tau + 1) * LANES, 0:S5_HALF] = blk(bl_re).T.astype(BF16)
            wl_ref[tau * LANES:(tau + 1) * LANES, S5_HALF:S5_STATE] = blk(bl_im).T.astype(BF16)
            kk = dot_exact(bl_re.T, c_re) - dot_exact(bl_im.T, c_im)
            toeplitz.append(jnp.where(same_group_sq, kk, 0.0).astype(BF16))
        if k >= 1:
            cl_re, cl_im = _cmul(p_re, p_im, c_re, c_im)
            t = k - 1
            ws_ref[0:S5_HALF, t * LANES:(t + 1) * LANES] = blk(cl_re).astype(BF16)
            ws_ref[S5_HALF:S5_STATE, t * LANES:(t + 1) * LANES] = (-blk(cl_im)).astype(BF16)
        p_re, p_im = _cmul(p_re, p_im, ab_re, ab_im)

    zero = jnp.zeros((LANES, LANES), BF16)
    for tau in range(S5_T):
        for t in range(S5_T):
            wt_ref[tau * LANES:(tau + 1) * LANES, t * LANES:(t + 1) * LANES] = (
                toeplitz[t - tau] if t >= tau else zero)

    a_re, a_im = _discretize(slre_ref[...], slim_ref[...], sldt_ref[...])
    step = 1
    while step < S5_T:
        a_re, a_im = _cmul(a_re, a_im, a_re, a_im)
        step *= 2
    at_ref[0:1, :] = a_re
    at_ref[1:2, :] = a_im


def _s5_prep(lam_re, lam_im, log_dt, b_re, b_im, c_re, c_im):
    depth, g, n = lam_re.shape
    p = b_re.shape[-1]
    assert (p, n) == (S5_P, S5_N) and S5_T & (S5_T - 1) == 0
    nt = g // S5_GPT

    def b_layout(a):
        return a.reshape(depth, nt, S5_GPT, n, p).transpose(0, 1, 3, 2, 4).reshape(depth, nt, n, LANES)

    def c_layout(a):
        return a.reshape(depth, nt, S5_GPT, p, n).transpose(0, 1, 4, 2, 3).reshape(depth, nt, n, LANES)

    def lam_layout(a):
        a = a.reshape(depth, nt, S5_GPT, n).transpose(0, 1, 3, 2)
        return jnp.broadcast_to(a[..., None], (depth, nt, n, S5_GPT, p)).reshape(depth, nt, n, LANES)

    ldt = jnp.broadcast_to(log_dt.reshape(depth, nt, 1, S5_GPT, 1), (depth, nt, n, S5_GPT, p))
    ldt = ldt.reshape(depth, nt, n, LANES)
    s_lre = lam_re.reshape(depth, nt, 1, S5_HALF)
    s_lim = lam_im.reshape(depth, nt, 1, S5_HALF)
    s_ldt = jnp.broadcast_to(log_dt.reshape(depth, nt, S5_GPT, 1), (depth, nt, S5_GPT, n))
    s_ldt = s_ldt.reshape(depth, nt, 1, S5_HALF)

    tile_spec = pl.BlockSpec((None, None, n, LANES), lambda l, j: (l, j, 0, 0))
    lane_spec = pl.BlockSpec((None, None, 1, S5_HALF), lambda l, j: (l, j, 0, 0))
    est = 2 * 2 * (S5_ROW * S5_ROW + 2 * S5_ROW * S5_STATE)
    return pl.pallas_call(
        _s5_prep_kernel,
        grid=(depth, nt),
        in_specs=[tile_spec] * 7 + [lane_spec] * 3,
        out_specs=[
            pl.BlockSpec((None, None, S5_ROW, S5_ROW), lambda l, j: (l, j, 0, 0)),
            pl.BlockSpec((None, None, S5_ROW, S5_STATE), lambda l, j: (l, j, 0, 0)),
            pl.BlockSpec((None, None, S5_STATE, S5_ROW), lambda l, j: (l, j, 0, 0)),
            pl.BlockSpec((None, None, 2, S5_HALF), lambda l, j: (l, j, 0, 0)),
        ],
        out_shape=[
            jax.ShapeDtypeStruct((depth, nt, S5_ROW, S5_ROW), BF16),
            jax.ShapeDtypeStruct((depth, nt, S5_ROW, S5_STATE), BF16),
            jax.ShapeDtypeStruct((depth, nt, S5_STATE, S5_ROW), BF16),
            jax.ShapeDtypeStruct((depth, nt, 2, S5_HALF), F32),
        ],
        compiler_params=_params(("parallel", "parallel"), est),
        name="s5_prep",
    )(b_layout(b_re), b_layout(b_im), c_layout(c_re), c_layout(c_im),
      lam_layout(lam_re), lam_layout(lam_im), ldt, s_lre, s_lim, s_ldt)


def _s5_kernel(u_ref, wt_ref, wl_ref, ws_ref, at_ref, d_ref, y_ref, loc_ref, st_ref, *, batch, chunks, pitch):
    x = u_ref[...]
    loc = jnp.dot(x, wl_ref[...], preferred_element_type=F32)
    slabs = S5_STATE // LANES
    half = slabs // 2
    for k in range(slabs):
        for b in range(batch):
            loc_ref[k, b * pitch:b * pitch + chunks, :] = loc[b * chunks:(b + 1) * chunks, k * LANES:(k + 1) * LANES]
    a_re = [at_ref[0:1, k * LANES:(k + 1) * LANES] for k in range(half)]
    a_im = [at_ref[1:2, k * LANES:(k + 1) * LANES] for k in range(half)]

    def body(c, carry):
        rows = pl.ds(c, batch, stride=pitch)
        new = [None] * slabs
        for k in range(half):
            s_re, s_im = carry[k], carry[half + k]
            st_ref[k, rows, :] = s_re
            st_ref[half + k, rows, :] = s_im
            new[k] = a_re[k] * s_re - a_im[k] * s_im + loc_ref[k, rows, :]
            new[half + k] = a_re[k] * s_im + a_im[k] * s_re + loc_ref[half + k, rows, :]
        return tuple(new)

    zero = jnp.zeros((batch, LANES), F32)
    lax.fori_loop(0, chunks, body, (zero,) * slabs, unroll=SUBLANES)

    y = jnp.dot(x, wt_ref[...], preferred_element_type=F32)
    st = jnp.concatenate(
        [jnp.concatenate([st_ref[k, b * pitch:b * pitch + chunks, :] for b in range(batch)], axis=0)
         for k in range(slabs)], axis=-1).astype(BF16)
    y = y + jnp.dot(st, ws_ref[...], preferred_element_type=F32)
    y_ref[...] = (y + d_ref[...] * x.astype(F32)).astype(BF16)


def _s5(u2, wt, wl, ws, at, d_rows, layer, batch, seq):
    nt, rows, _ = u2.shape
    chunks = seq // S5_T
    assert rows == batch * chunks and chunks % SUBLANES == 0
    pitch = chunks + SUBLANES if (chunks // SUBLANES) % 2 == 0 else chunks
    est = (2 * 2 * rows * S5_ROW * 2 + 2 * 2 * (S5_ROW * S5_ROW + 2 * S5_ROW * S5_STATE)
           + 2 * batch * pitch * S5_STATE * 4 + 3 * rows * S5_ROW * 4 + rows * S5_STATE * 2)
    return pl.pallas_call(
        functools.partial(_s5_kernel, batch=batch, chunks=chunks, pitch=pitch),
        grid=(nt,),
        in_specs=[
            pl.BlockSpec((None, rows, S5_ROW), lambda j: (j, 0, 0)),
            pl.BlockSpec((None, None, S5_ROW, S5_ROW), lambda j: (layer, j, 0, 0)),
            pl.BlockSpec((None, None, S5_ROW, S5_STATE), lambda j: (layer, j, 0, 0)),
            pl.BlockSpec((None, None, S5_STATE, S5_ROW), lambda j: (layer, j, 0, 0)),
            pl.BlockSpec((None, None, 2, S5_HALF), lambda j: (layer, j, 0, 0)),
            pl.BlockSpec((None, None, 1, S5_ROW), lambda j: (layer, j, 0, 0)),
        ],
        out_specs=pl.BlockSpec((None, rows, S5_ROW), lambda j: (j, 0, 0)),
        out_shape=jax.ShapeDtypeStruct((nt, rows, S5_ROW), BF16),
        scratch_shapes=[pltpu.VMEM((S5_STATE // LANES, batch * pitch, LANES), F32)] * 2,
        compiler_params=_params(("parallel",), est),
        name="s5_mix",
    )(u2, wt, wl, ws, at, d_rows)


def _out_kernel(y_ref, z_ref, x_ref, wglu_ref, wout_ref, gs_ref, g_ref, b_ref, *rest, alpha, cast_next):
    if cast_next:
        nin_ref, o_ref, cin_ref, ys_ref = rest
        cin_ref[...] = nin_ref[...].astype(BF16)
    else:
        o_ref, ys_ref = rest
    nt = y_ref.shape[0]
    tm = x_ref.shape[0]
    for j in range(nt):
        for t in range(S5_T):
            ys_ref[j, pl.ds(t, tm // S5_T, stride=S5_T), :] = y_ref[j, :, t * LANES:(t + 1) * LANES].astype(F32)
    y = jnp.concatenate([ys_ref[j] for j in range(nt)], axis=-1)
    y = _gelu_tanh(y)
    y = y * _sigmoid(jnp.dot(y.astype(BF16), wglu_ref[...], preferred_element_type=F32))
    y = _rms_norm(y, gs_ref[...])
    yz = jnp.concatenate([y.astype(BF16), z_ref[...]], axis=-1)
    mix = jnp.dot(yz, wout_ref[...], preferred_element_type=F32)
    o_ref[...] = _layer_norm(alpha * x_ref[...] + mix, g_ref[...], b_ref[...])


def _out(y2, z, x, w_glu, w_out, g_s5, ln_g, ln_b, layer, alpha, next_w_in=None, next_layer=0, tm=512):
    m, d = x.shape
    nt = y2.shape[0]
    d_s5 = nt * LANES
    d_conv = z.shape[1]
    steps = m // tm
    cast_next = next_w_in is not None
    assert tm % (S5_T * 2 * SUBLANES) == 0
    est = (2 * 2 * tm * d * 4 + 2 * tm * (d_s5 + d_conv) * 2
           + (d_s5 * d_s5 + (d_s5 + d_conv) * d) * 2
           + tm * d_s5 * 4 + 3 * tm * d_s5 * 4 + tm * d * 4)
    in_specs = [
        pl.BlockSpec((nt, tm // S5_T, S5_ROW), lambda i: (0, i, 0)),
        pl.BlockSpec((tm, d_conv), lambda i: (i, 0)),
        pl.BlockSpec((tm, d), lambda i: (i, 0)),
        pl.BlockSpec((None, d_s5, d_s5), lambda i: (0, 0, 0), pipeline_mode=pl.Buffered(1)),
        pl.BlockSpec((None, d_s5 + d_conv, d), lambda i: (0, 0, 0), pipeline_mode=pl.Buffered(1)),
        pl.BlockSpec((None, 1, d_s5), lambda i: (layer, 0, 0)),
        pl.BlockSpec((None, 1, d), lambda i: (layer, 0, 0)),
        pl.BlockSpec((None, 1, d), lambda i: (layer, 0, 0)),
    ]
    out_specs = [pl.BlockSpec((tm, d), lambda i: (i, 0))]
    out_shape = [jax.ShapeDtypeStruct((m, d), F32)]
    args = [y2, z, x, w_glu, w_out, g_s5, ln_g, ln_b]
    if cast_next:
        rows, cols = next_w_in.shape[1] // steps, next_w_in.shape[2]
        assert next_w_in.shape[1] % (steps * 2 * SUBLANES) == 0
        in_specs.append(pl.BlockSpec((None, rows, cols), lambda i: (next_layer, i, 0)))
        out_specs.append(pl.BlockSpec((None, rows, cols), lambda i: (0, i, 0)))
        out_shape.append(jax.ShapeDtypeStruct((1,) + next_w_in.shape[1:], BF16))
        args.append(next_w_in)
        est += 2 * rows * cols * 6
    outs = pl.pallas_call(
        functools.partial(_out_kernel, alpha=alpha, cast_next=cast_next),
        grid=(steps,),
        in_specs=in_specs,
        out_specs=out_specs,
        out_shape=out_shape,
        scratch_shapes=[pltpu.VMEM((nt, tm, LANES), F32)],
        compiler_params=_params(("arbitrary",), est),
        name="glu_out_cast" if cast_next else "glu_out",
    )(*args)
    return outs[0], (outs[1] if cast_next else None)


def kernel(x, ffn1_gate, ffn1_up, ffn1_down, ln1_g, ln1_b, w_in, s5_lam_re, s5_lam_im, s5_log_dt,
           s5_b_re, s5_b_im, s5_c_re, s5_c_im, s5_d, s5_w_glu, conv_w, conv_b, g_s5, g_conv, w_out,
           ln2_g, ln2_b, ffn2_gate, ffn2_up, ffn2_down, ln3_g, ln3_b):
    batch, seq, d_model = x.shape
    depth = ffn1_gate.shape[0]
    d_s5 = s5_w_glu.shape[-1]
    alpha = (2.0 * depth) ** 0.25
    assert d_s5 % LANES == 0 and seq % S5_T == 0

    def row(a):
        return a.reshape(depth, 1, a.shape[-1])

    ffns = []
    for l in range(depth):
        ffns.append(((ffn1_gate, ffn1_up, ffn1_down), row(ln1_g), row(ln1_b), l))
        ffns.append(((ffn2_gate, ffn2_up, ffn2_down), row(ln3_g), row(ln3_b), l))
    w_in_b = w_in[0:1].astype(BF16)

    wt, wl, ws, at = _s5_prep(s5_lam_re, s5_lam_im, s5_log_dt, s5_b_re, s5_b_im, s5_c_re, s5_c_im)
    nt = d_s5 // LANES
    d_rows = jnp.tile(s5_d.reshape(depth, nt, 1, LANES), (1, 1, 1, S5_T))

    state = {"h": x.reshape(batch * seq, d_model), "weights": ffns[0][0], "wlayer": 0, "k": 0}

    def ffn():
        k = state["k"]
        _, g, b, l = ffns[k]
        nxt = ffns[k + 1] if k + 1 < len(ffns) else None
        state["h"], cast = _ffn(state["h"], state["weights"], state["wlayer"], g, b, l, alpha,
                                nxt[0] if nxt else None, nxt[3] if nxt else 0)
        state["weights"], state["wlayer"], state["k"] = cast, 0, k + 1

    for l in range(depth):
        ffn()
        h = state["h"]
        u2, z, w_glu_b, w_out_b = _proj(h, w_in_b, conv_w, row(conv_b), row(g_conv), s5_w_glu, w_out,
                                        l, seq, d_s5)
        y2 = _s5(u2, wt, wl, ws, at, d_rows, l, batch, seq)
        state["h"], w_in_b = _out(y2, z, h, w_glu_b, w_out_b, row(g_s5), row(ln2_g), row(ln2_b), l, alpha,
                                  w_in if l + 1 < depth else None, l + 1)
        ffn()
    return state["h"].reshape(batch, seq, d_model)
```

```python
import functools

import jax
import jax.numpy as jnp
from jax import lax
from jax.experimental import pallas as pl
from jax.experimental.pallas import tpu as pltpu

F32 = jnp.float32
BF16 = jnp.bfloat16

LN_EPS = 1e-5
RMS_EPS = 1e-6
CONV_W = 3
FFN_BETA = 0.5

LANES = 128
SUBLANES = 8
V7X_VMEM_BYTES = 64 * 1024 * 1024
VMEM_CAP_BYTES = V7X_VMEM_BYTES - 2 * 1024 * 1024
COMPILER_TEMP_BYTES = 8 * 1024 * 1024

S5_P = 16
S5_N = 64
S5_T = 8
S5_GPT = LANES // S5_P
S5_HALF = S5_GPT * S5_N
S5_STATE = 2 * S5_HALF
S5_ROW = S5_T * LANES


def _layer_norm(y, g, b, eps=LN_EPS):
    mu = jnp.mean(y, axis=-1, keepdims=True)
    yc = y - mu
    var = jnp.mean(yc * yc, axis=-1, keepdims=True)
    return yc * lax.rsqrt(var + eps) * g + b


def _rms_norm(y, g):
    return y * lax.rsqrt(jnp.mean(y * y, axis=-1, keepdims=True) + RMS_EPS) * g


def _sigmoid(x):
    return 1.0 / (1.0 + jnp.exp(-x))


def _gelu_tanh(x):
    c = 0.7978845608028654
    return 0.5 * x * (1.0 + jnp.tanh(c * (x + 0.044715 * (x * x * x))))


def _cmul(ar, ai, br, bi):
    return ar * br - ai * bi, ar * bi + ai * br


def _params(sem, est_bytes):
    limit = min(VMEM_CAP_BYTES, est_bytes + COMPILER_TEMP_BYTES)
    return pltpu.CompilerParams(dimension_semantics=sem, vmem_limit_bytes=limit)


def _ffn_kernel(x_ref, wg_ref, wu_ref, wd_ref, g_ref, b_ref, *rest, alpha, cast_next):
    if cast_next:
        ng_ref, nu_ref, nd_ref, o_ref, cg_ref, cu_ref, cd_ref, xb_ref = rest
    else:
        o_ref, xb_ref = rest
    f = pl.program_id(1)
    last = pl.num_programs(1) - 1

    def weights():
        if cast_next:
            cg_ref[...] = ng_ref[...].astype(BF16)
            cu_ref[...] = nu_ref[...].astype(BF16)
            cd_ref[...] = nd_ref[...].astype(BF16)
        return wg_ref[...].astype(BF16), wu_ref[...].astype(BF16), wd_ref[...].astype(BF16)

    def swiglu_down(xb, wg, wu, wd):
        gate = jnp.dot(xb, wg, preferred_element_type=F32)
        up = jnp.dot(xb, wu, preferred_element_type=F32)
        h = (gate * _sigmoid(gate)) * up
        return jnp.dot(h.astype(BF16), wd, preferred_element_type=F32)

    @pl.when(f == 0)
    def _():
        x = x_ref[...]
        xb = x.astype(BF16)
        xb_ref[...] = xb
        o_ref[...] = (alpha / FFN_BETA) * x + swiglu_down(xb, *weights())

    @pl.when(jnp.logical_and(f > 0, f < last))
    def _():
        o_ref[...] += swiglu_down(xb_ref[...], *weights())

    @pl.when(f == last)
    def _():
        acc = o_ref[...] + swiglu_down(xb_ref[...], *weights())
        o_ref[...] = _layer_norm(acc, g_ref[...], b_ref[...], LN_EPS / (FFN_BETA * FFN_BETA))


def _ffn(x, weights, wlayer, ln_g, ln_b, layer, alpha, next_weights=None, next_layer=0, tm=1024):
    wg, wu, wd = weights
    m, d = x.shape
    ff = wg.shape[-1]
    wbytes = wg.dtype.itemsize
    tf = 2 * LANES * (4 // wbytes)
    ni, nf = m // tm, ff // tf
    cast_next = next_weights is not None
    dr = d // ni
    assert ff % tf == 0 and nf >= 2 and m % tm == 0 and d % (ni * LANES) == 0
    est = (2 * 2 * tm * d * 4 + 2 * 3 * d * tf * wbytes + tm * d * 2 + (wbytes // 4) * 3 * d * tf * 2
           + 2 * tm * tf * 4 + tm * tf * 2 + cast_next * 2 * 3 * dr * tf * 6)
    in_specs = [
        pl.BlockSpec((tm, d), lambda i, f: (i, 0)),
        pl.BlockSpec((None, d, tf), lambda i, f: (wlayer, 0, f)),
        pl.BlockSpec((None, d, tf), lambda i, f: (wlayer, 0, f)),
        pl.BlockSpec((None, tf, d), lambda i, f: (wlayer, f, 0)),
        pl.BlockSpec((None, 1, d), lambda i, f: (layer, 0, 0)),
        pl.BlockSpec((None, 1, d), lambda i, f: (layer, 0, 0)),
    ]
    out_specs = [pl.BlockSpec((tm, d), lambda i, f: (i, 0))]
    out_shape = [jax.ShapeDtypeStruct((m, d), F32)]
    args = [x, wg, wu, wd, ln_g, ln_b]
    if cast_next:
        in_specs += [
            pl.BlockSpec((None, dr, tf), lambda i, f: (next_layer, i, f)),
            pl.BlockSpec((None, dr, tf), lambda i, f: (next_layer, i, f)),
            pl.BlockSpec((None, tf, dr), lambda i, f: (next_layer, f, i)),
        ]
        out_specs += [
            pl.BlockSpec((None, dr, tf), lambda i, f: (0, i, f)),
            pl.BlockSpec((None, dr, tf), lambda i, f: (0, i, f)),
            pl.BlockSpec((None, tf, dr), lambda i, f: (0, f, i)),
        ]
        out_shape += [jax.ShapeDtypeStruct((1,) + w.shape[1:], BF16) for w in next_weights]
        args += list(next_weights)
    outs = pl.pallas_call(
        functools.partial(_ffn_kernel, alpha=alpha, cast_next=cast_next),
        grid=(ni, nf),
        in_specs=in_specs,
        out_specs=out_specs,
        out_shape=out_shape,
        scratch_shapes=[pltpu.VMEM((tm, d), BF16)],
        compiler_params=_params(("arbitrary", "arbitrary"), est),
        name="ffn_cast" if cast_next else "ffn",
    )(*args)
    return outs[0], tuple(outs[1:])


def _proj_kernel(x_ref, win_ref, cw_ref, cb_ref, gc_ref, nglu_ref, nout_ref,
                 u_ref, z_ref, cglu_ref, cout_ref, us_ref, vpad_ref, *, tiles_per_seq):
    tm = x_ref.shape[0]
    nt = u_ref.shape[0]
    d_s5 = nt * LANES
    d_conv = z_ref.shape[1]

    @pl.when(pl.program_id(0) % tiles_per_seq == 0)
    def _():
        vpad_ref[0:SUBLANES, :] = jnp.zeros((SUBLANES, d_conv), F32)

    cglu_ref[...] = nglu_ref[...].astype(BF16)
    cout_ref[...] = nout_ref[...].astype(BF16)

    xb = x_ref[...].astype(BF16)

    u = jnp.dot(xb, win_ref[:, 0:d_s5], preferred_element_type=F32)
    for j in range(nt):
        us_ref[j] = u[:, j * LANES:(j + 1) * LANES]
    for j in range(nt):
        for t in range(S5_T):
            rows = us_ref[j, pl.ds(t, tm // S5_T, stride=S5_T), :]
            u_ref[j, :, t * LANES:(t + 1) * LANES] = rows.astype(BF16)

    o = d_s5
    gate_c = jnp.dot(xb, win_ref[:, o + d_conv:o + 2 * d_conv], preferred_element_type=F32)
    hid = jnp.dot(xb, win_ref[:, o + 2 * d_conv:o + 3 * d_conv], preferred_element_type=F32)
    v = gate_c * hid
    vpad_ref[SUBLANES:SUBLANES + tm, :] = v
    v1 = vpad_ref[SUBLANES - 1:SUBLANES - 1 + tm, :]
    v2 = vpad_ref[SUBLANES - 2:SUBLANES - 2 + tm, :]
    conv = cb_ref[...] + cw_ref[0:1, :] * v2 + cw_ref[1:2, :] * v1 + cw_ref[2:3, :] * v
    vpad_ref[0:SUBLANES, :] = vpad_ref[tm:tm + SUBLANES, :]

    gate_b = jnp.dot(xb, win_ref[:, o:o + d_conv], preferred_element_type=F32)
    z_ref[...] = _rms_norm(gate_b * conv, gc_ref[...]).astype(BF16)


def _proj(x, w_in, conv_w, conv_b, g_conv, w_glu, w_out, layer, seq, d_s5, tm=512):
    m, d = x.shape
    d_in = w_in.shape[-1]
    d_conv = conv_b.shape[-1]
    nt = d_s5 // LANES
    steps = m // tm
    rg, ro = w_glu.shape[1] // steps, w_out.shape[1] // steps
    assert CONV_W - 1 <= SUBLANES and conv_w.shape[1] == CONV_W and seq % tm == 0
    assert tm % (S5_T * 2 * SUBLANES) == 0
    assert w_glu.shape[1] % (steps * 2 * SUBLANES) == 0 and w_out.shape[1] % (steps * 2 * SUBLANES) == 0
    est = (2 * tm * d * 4 + d * d_in * 2 + 2 * tm * (d_s5 + d_conv) * 2
           + tm * d_s5 * 4 + (tm + SUBLANES) * d_conv * 4 + tm * d * 2 + 6 * tm * d_conv * 4
           + 2 * (rg * w_glu.shape[2] + ro * w_out.shape[2]) * 6)
    return pl.pallas_call(
        functools.partial(_proj_kernel, tiles_per_seq=seq // tm),
        grid=(steps,),
        in_specs=[
            pl.BlockSpec((tm, d), lambda i: (i, 0)),
            pl.BlockSpec((None, d, d_in), lambda i: (0, 0, 0), pipeline_mode=pl.Buffered(1)),
            pl.BlockSpec((None, CONV_W, d_conv), lambda i: (layer, 0, 0)),
            pl.BlockSpec((None, 1, d_conv), lambda i: (layer, 0, 0)),
            pl.BlockSpec((None, 1, d_conv), lambda i: (layer, 0, 0)),
            pl.BlockSpec((None, rg, w_glu.shape[2]), lambda i: (layer, i, 0)),
            pl.BlockSpec((None, ro, w_out.shape[2]), lambda i: (layer, i, 0)),
        ],
        out_specs=[
            pl.BlockSpec((nt, tm // S5_T, S5_ROW), lambda i: (0, i, 0)),
            pl.BlockSpec((tm, d_conv), lambda i: (i, 0)),
            pl.BlockSpec((None, rg, w_glu.shape[2]), lambda i: (0, i, 0)),
            pl.BlockSpec((None, ro, w_out.shape[2]), lambda i: (0, i, 0)),
        ],
        out_shape=[
            jax.ShapeDtypeStruct((nt, m // S5_T, S5_ROW), BF16),
            jax.ShapeDtypeStruct((m, d_conv), BF16),
            jax.ShapeDtypeStruct((1,) + w_glu.shape[1:], BF16),
            jax.ShapeDtypeStruct((1,) + w_out.shape[1:], BF16),
        ],
        scratch_shapes=[pltpu.VMEM((nt, tm, LANES), F32), pltpu.VMEM((tm + SUBLANES, d_conv), F32)],
        compiler_params=_params(("arbitrary",), est),
        name="proj_conv",
    )(x, w_in, conv_w, conv_b, g_conv, w_glu, w_out)


def _discretize(lre, lim, ldt):
    dt = jnp.exp(ldt)
    mag = jnp.exp(lre * dt)
    ang = lim * dt
    return mag * jnp.cos(ang), mag * jnp.sin(ang)


def _s5_prep_kernel(bre_ref, bim_ref, cre_ref, cim_ref, lre_ref, lim_ref, ldt_ref,
                    slre_ref, slim_ref, sldt_ref, wt_ref, wl_ref, ws_ref, at_ref):
    lre, lim = lre_ref[...], lim_ref[...]
    ab_re, ab_im = _discretize(lre, lim, ldt_ref[...])
    den = lre * lre + lim * lim
    nr, ni = ab_re - 1.0, ab_im
    q_re = (nr * lre + ni * lim) / den
    q_im = (ni * lre - nr * lim) / den
    b_re, b_im = bre_ref[...], bim_ref[...]
    bb_re = q_re * b_re - q_im * b_im
    bb_im = q_re * b_im + q_im * b_re
    c_re, c_im = cre_ref[...], cim_ref[...]

    rows = lax.broadcasted_iota(jnp.int32, (S5_HALF, LANES), 0) // S5_N
    cols = lax.broadcasted_iota(jnp.int32, (S5_HALF, LANES), 1) // S5_P
    same_group = rows == cols
    same_group_sq = (lax.broadcasted_iota(jnp.int32, (LANES, LANES), 0) // S5_P
                     == lax.broadcasted_iota(jnp.int32, (LANES, LANES), 1) // S5_P)

    def blk(a):
        return jnp.where(same_group, jnp.concatenate([a] * S5_GPT, axis=0), 0.0)

    def dot_exact(a, b):
        return jnp.dot(a, b, precision=lax.Precision.HIGHEST, preferred_element_type=F32)

    p_re, p_im = jnp.ones_like(ab_re), jnp.zeros_like(ab_re)
    toeplitz = []
    for k in range(S5_T + 1):
        if k < S5_T:
            bl_re, bl_im = _cmul(p_re, p_im, bb_re, bb_im)
            tau = S5_T - 1 - k
            wl_ref[tau * LANES:(tau + 1) * LANES, 0:S5_HALF] = blk(bl_re).T.astype(BF16)
            wl_ref[tau * LANES:(tau + 1) * LANES, S5_HALF:S5_STATE] = blk(bl_im).T.astype(BF16)
            kk = dot_exact(bl_re.T, c_re) - dot_exact(bl_im.T, c_im)
            toeplitz.append(jnp.where(same_group_sq, kk, 0.0).astype(BF16))
        if k >= 1:
            cl_re, cl_im = _cmul(p_re, p_im, c_re, c_im)
            t = k - 1
            ws_ref[0:S5_HALF, t * LANES:(t + 1) * LANES] = blk(cl_re).astype(BF16)
            ws_ref[S5_HALF:S5_STATE, t * LANES:(t + 1) * LANES] = (-blk(cl_im)).astype(BF16)
        p_re, p_im = _cmul(p_re, p_im, ab_re, ab_im)

    zero = jnp.zeros((LANES, LANES), BF16)
    for tau in range(S5_T):
        for t in range(S5_T):
            wt_ref[tau * LANES:(tau + 1) * LANES, t * LANES:(t + 1) * LANES] = (
                toeplitz[t - tau] if t >= tau else zero)

    a_re, a_im = _discretize(slre_ref[...], slim_ref[...], sldt_ref[...])
    step = 1
    while step < S5_T:
        a_re, a_im = _cmul(a_re, a_im, a_re, a_im)
        step *= 2
    at_ref[0:1, :] = a_re
    at_ref[1:2, :] = a_im


def _s5_prep(lam_re, lam_im, log_dt, b_re, b_im, c_re, c_im):
    depth, g, n = lam_re.shape
    p = b_re.shape[-1]
    assert (p, n) == (S5_P, S5_N) and S5_T & (S5_T - 1) == 0
    nt = g // S5_GPT

    def b_layout(a):
        return a.reshape(depth, nt, S5_GPT, n, p).transpose(0, 1, 3, 2, 4).reshape(depth, nt, n, LANES)

    def c_layout(a):
        return a.reshape(depth, nt, S5_GPT, p, n).transpose(0, 1, 4, 2, 3).reshape(depth, nt, n, LANES)

    def lam_layout(a):
        a = a.reshape(depth, nt, S5_GPT, n).transpose(0, 1, 3, 2)
        return jnp.broadcast_to(a[..., None], (depth, nt, n, S5_GPT, p)).reshape(depth, nt, n, LANES)

    ldt = jnp.broadcast_to(log_dt.reshape(depth, nt, 1, S5_GPT, 1), (depth, nt, n, S5_GPT, p))
    ldt = ldt.reshape(depth, nt, n, LANES)
    s_lre = lam_re.reshape(depth, nt, 1, S5_HALF)
    s_lim = lam_im.reshape(depth, nt, 1, S5_HALF)
    s_ldt = jnp.broadcast_to(log_dt.reshape(depth, nt, S5_GPT, 1), (depth, nt, S5_GPT, n))
    s_ldt = s_ldt.reshape(depth, nt, 1, S5_HALF)

    tile_spec = pl.BlockSpec((None, None, n, LANES), lambda l, j: (l, j, 0, 0))
    lane_spec = pl.BlockSpec((None, None, 1, S5_HALF), lambda l, j: (l, j, 0, 0))
    est = 2 * 2 * (S5_ROW * S5_ROW + 2 * S5_ROW * S5_STATE)
    return pl.pallas_call(
        _s5_prep_kernel,
        grid=(depth, nt),
        in_specs=[tile_spec] * 7 + [lane_spec] * 3,
        out_specs=[
            pl.BlockSpec((None, None, S5_ROW, S5_ROW), lambda l, j: (l, j, 0, 0)),
            pl.BlockSpec((None, None, S5_ROW, S5_STATE), lambda l, j: (l, j, 0, 0)),
            pl.BlockSpec((None, None, S5_STATE, S5_ROW), lambda l, j: (l, j, 0, 0)),
            pl.BlockSpec((None, None, 2, S5_HALF), lambda l, j: (l, j, 0, 0)),
        ],
        out_shape=[
            jax.ShapeDtypeStruct((depth, nt, S5_ROW, S5_ROW), BF16),
            jax.ShapeDtypeStruct((depth, nt, S5_ROW, S5_STATE), BF16),
            jax.ShapeDtypeStruct((depth, nt, S5_STATE, S5_ROW), BF16),
            jax.ShapeDtypeStruct((depth, nt, 2, S5_HALF), F32),
        ],
        compiler_params=_params(("parallel", "parallel"), est),
        name="s5_prep",
    )(b_layout(b_re), b_layout(b_im), c_layout(c_re), c_layout(c_im),
      lam_layout(lam_re), lam_layout(lam_im), ldt, s_lre, s_lim, s_ldt)


def _s5_kernel(u_ref, wt_ref, wl_ref, ws_ref, at_ref, d_ref, y_ref, loc_ref, st_ref, *, batch, chunks, pitch):
    x = u_ref[...]
    loc = jnp.dot(x, wl_ref[...], preferred_element_type=F32)
    slabs = S5_STATE // LANES
    half = slabs // 2
    for k in range(slabs):
        for b in range(batch):
            loc_ref[k, b * pitch:b * pitch + chunks, :] = loc[b * chunks:(b + 1) * chunks, k * LANES:(k + 1) * LANES]
    a_re = [at_ref[0:1, k * LANES:(k + 1) * LANES] for k in range(half)]
    a_im = [at_ref[1:2, k * LANES:(k + 1) * LANES] for k in range(half)]

    def body(c, carry):
        rows = pl.ds(c, batch, stride=pitch)
        new = [None] * slabs
        for k in range(half):
            s_re, s_im = carry[k], carry[half + k]
            st_ref[k, rows, :] = s_re
            st_ref[half + k, rows, :] = s_im
            new[k] = a_re[k] * s_re - a_im[k] * s_im + loc_ref[k, rows, :]
            new[half + k] = a_re[k] * s_im + a_im[k] * s_re + loc_ref[half + k, rows, :]
        return tuple(new)

    carry = (jnp.zeros((batch, LANES), F32),) * slabs
    for c in range(chunks):
        carry = body(c, carry)

    pair = 2 * LANES
    y = jnp.concatenate(
        [jnp.dot(x[:, 0:(c + 1) * pair], wt_ref[0:(c + 1) * pair, c * pair:(c + 1) * pair],
                 preferred_element_type=F32) for c in range(S5_ROW // pair)], axis=-1)
    st = jnp.concatenate(
        [jnp.concatenate([st_ref[k, b * pitch:b * pitch + chunks, :] for b in range(batch)], axis=0)
         for k in range(slabs)], axis=-1).astype(BF16)
    y = y + jnp.dot(st, ws_ref[...], preferred_element_type=F32)
    y_ref[...] = (y + d_ref[...] * x.astype(F32)).astype(BF16)


def _s5(u2, wt, wl, ws, at, d_rows, layer, batch, seq):
    nt, rows, _ = u2.shape
    chunks = seq // S5_T
    assert rows == batch * chunks and chunks % SUBLANES == 0
    pitch = chunks + SUBLANES if (chunks // SUBLANES) % 2 == 0 else chunks
    est = (2 * 2 * rows * S5_ROW * 2 + 2 * 2 * (S5_ROW * S5_ROW + 2 * S5_ROW * S5_STATE)
           + 2 * batch * pitch * S5_STATE * 4 + 3 * rows * S5_ROW * 4 + rows * S5_STATE * 2)
    return pl.pallas_call(
        functools.partial(_s5_kernel, batch=batch, chunks=chunks, pitch=pitch),
        grid=(nt,),
        in_specs=[
            pl.BlockSpec((None, rows, S5_ROW), lambda j: (j, 0, 0)),
            pl.BlockSpec((None, None, S5_ROW, S5_ROW), lambda j: (layer, j, 0, 0)),
            pl.BlockSpec((None, None, S5_ROW, S5_STATE), lambda j: (layer, j, 0, 0)),
            pl.BlockSpec((None, None, S5_STATE, S5_ROW), lambda j: (layer, j, 0, 0)),
            pl.BlockSpec((None, None, 2, S5_HALF), lambda j: (layer, j, 0, 0)),
            pl.BlockSpec((None, None, 1, S5_ROW), lambda j: (layer, j, 0, 0)),
        ],
        out_specs=pl.BlockSpec((None, rows, S5_ROW), lambda j: (j, 0, 0)),
        out_shape=jax.ShapeDtypeStruct((nt, rows, S5_ROW), BF16),
        scratch_shapes=[pltpu.VMEM((S5_STATE // LANES, batch * pitch, LANES), F32)] * 2,
        compiler_params=_params(("parallel",), est),
        name="s5_mix",
    )(u2, wt, wl, ws, at, d_rows)


def _out_kernel(y_ref, z_ref, x_ref, wglu_ref, wout_ref, gs_ref, g_ref, b_ref, *rest, alpha, cast_next):
    if cast_next:
        nin_ref, o_ref, cin_ref, ys_ref = rest
        cin_ref[...] = nin_ref[...].astype(BF16)
    else:
        o_ref, ys_ref = rest
    nt = y_ref.shape[0]
    tm = x_ref.shape[0]
    for j in range(nt):
        for t in range(S5_T):
            ys_ref[j, pl.ds(t, tm // S5_T, stride=S5_T), :] = y_ref[j, :, t * LANES:(t + 1) * LANES].astype(F32)
    y = jnp.concatenate([ys_ref[j] for j in range(nt)], axis=-1)
    y = _gelu_tanh(y)
    y = y * _sigmoid(jnp.dot(y.astype(BF16), wglu_ref[...], preferred_element_type=F32))
    y = _rms_norm(y, gs_ref[...])
    yz = jnp.concatenate([y.astype(BF16), z_ref[...]], axis=-1)
    mix = jnp.dot(yz, wout_ref[...], preferred_element_type=F32)
    o_ref[...] = _layer_norm(alpha * x_ref[...] + mix, g_ref[...], b_ref[...])


def _out(y2, z, x, w_glu, w_out, g_s5, ln_g, ln_b, layer, alpha, next_w_in=None, next_layer=0, tm=512):
    m, d = x.shape
    nt = y2.shape[0]
    d_s5 = nt * LANES
    d_conv = z.shape[1]
    steps = m // tm
    cast_next = next_w_in is not None
    assert tm % (S5_T * 2 * SUBLANES) == 0
    est = (2 * 2 * tm * d * 4 + 2 * tm * (d_s5 + d_conv) * 2
           + (d_s5 * d_s5 + (d_s5 + d_conv) * d) * 2
           + tm * d_s5 * 4 + 3 * tm * d_s5 * 4 + tm * d * 4)
    in_specs = [
        pl.BlockSpec((nt, tm // S5_T, S5_ROW), lambda i: (0, i, 0)),
        pl.BlockSpec((tm, d_conv), lambda i: (i, 0)),
        pl.BlockSpec((tm, d), lambda i: (i, 0)),
        pl.BlockSpec((None, d_s5, d_s5), lambda i: (0, 0, 0), pipeline_mode=pl.Buffered(1)),
        pl.BlockSpec((None, d_s5 + d_conv, d), lambda i: (0, 0, 0), pipeline_mode=pl.Buffered(1)),
        pl.BlockSpec((None, 1, d_s5), lambda i: (layer, 0, 0)),
        pl.BlockSpec((None, 1, d), lambda i: (layer, 0, 0)),
        pl.BlockSpec((None, 1, d), lambda i: (layer, 0, 0)),
    ]
    out_specs = [pl.BlockSpec((tm, d), lambda i: (i, 0))]
    out_shape = [jax.ShapeDtypeStruct((m, d), F32)]
    args = [y2, z, x, w_glu, w_out, g_s5, ln_g, ln_b]
    if cast_next:
        rows, cols = next_w_in.shape[1] // steps, next_w_in.shape[2]
        assert next_w_in.shape[1] % (steps * 2 * SUBLANES) == 0
        in_specs.append(pl.BlockSpec((None, rows, cols), lambda i: (next_layer, i, 0)))
        out_specs.append(pl.BlockSpec((None, rows, cols), lambda i: (0, i, 0)))
        out_shape.append(jax.ShapeDtypeStruct((1,) + next_w_in.shape[1:], BF16))
        args.append(next_w_in)
        est += 2 * rows * cols * 6
    outs = pl.pallas_call(
        functools.partial(_out_kernel, alpha=alpha, cast_next=cast_next),
        grid=(steps,),
        in_specs=in_specs,
        out_specs=out_specs,
        out_shape=out_shape,
        scratch_shapes=[pltpu.VMEM((nt, tm, LANES), F32)],
        compiler_params=_params(("arbitrary",), est),
        name="glu_out_cast" if cast_next else "glu_out",
    )(*args)
    return outs[0], (outs[1] if cast_next else None)


def kernel(x, ffn1_gate, ffn1_up, ffn1_down, ln1_g, ln1_b, w_in, s5_lam_re, s5_lam_im, s5_log_dt,
           s5_b_re, s5_b_im, s5_c_re, s5_c_im, s5_d, s5_w_glu, conv_w, conv_b, g_s5, g_conv, w_out,
           ln2_g, ln2_b, ffn2_gate, ffn2_up, ffn2_down, ln3_g, ln3_b):
    batch, seq, d_model = x.shape
    depth = ffn1_gate.shape[0]
    d_s5 = s5_w_glu.shape[-1]
    alpha = (2.0 * depth) ** 0.25
    assert d_s5 % LANES == 0 and seq % S5_T == 0

    def row(a):
        return a.reshape(depth, 1, a.shape[-1])

    ffns = []
    for l in range(depth):
        ffns.append(((ffn1_gate, ffn1_up, ffn1_down), row(ln1_g), row(ln1_b), l))
        ffns.append(((ffn2_gate, ffn2_up, ffn2_down), row(ln3_g), row(ln3_b), l))
    w_in_b = w_in[0:1].astype(BF16)

    wt, wl, ws, at = _s5_prep(s5_lam_re, s5_lam_im, s5_log_dt, s5_b_re, s5_b_im, s5_c_re, s5_c_im)
    nt = d_s5 // LANES
    d_rows = jnp.tile(s5_d.reshape(depth, nt, 1, LANES), (1, 1, 1, S5_T))

    state = {"h": x.reshape(batch * seq, d_model), "weights": ffns[0][0], "wlayer": 0, "k": 0}

    def ffn():
        k = state["k"]
        _, g, b, l = ffns[k]
        nxt = ffns[k + 1] if k + 1 < len(ffns) else None
        state["h"], cast = _ffn(state["h"], state["weights"], state["wlayer"], g, b, l, alpha,
                                nxt[0] if nxt else None, nxt[3] if nxt else 0)
        state["weights"], state["wlayer"], state["k"] = cast, 0, k + 1

    for l in range(depth):
        ffn()
        h = state["h"]
        u2, z, w_glu_b, w_out_b = _proj(h, w_in_b, conv_w, row(conv_b), row(g_conv), s5_w_glu, w_out,
                                        l, seq, d_s5)
        y2 = _s5(u2, wt, wl, ws, at, d_rows, l, batch, seq)
        state["h"], w_in_b = _out(y2, z, h, w_glu_b, w_out_b, row(g_s5), row(ln2_g), row(ln2_b), l, alpha,
                                  w_in if l + 1 < depth else None, l + 1)
        ffn()
    return state["h"].reshape(batch, seq, d_model)
```

```python
import functools

import jax
import jax.numpy as jnp
from jax import lax
from jax.experimental import pallas as pl
from jax.experimental.pallas import tpu as pltpu

F32 = jnp.float32
BF16 = jnp.bfloat16

LN_EPS = 1e-5
RMS_EPS = 1e-6
CONV_W = 3
FFN_BETA = 0.5

LANES = 128
SUBLANES = 8
V7X_VMEM_BYTES = 64 * 1024 * 1024
VMEM_CAP_BYTES = V7X_VMEM_BYTES - 2 * 1024 * 1024
COMPILER_TEMP_BYTES = 8 * 1024 * 1024

S5_P = 16
S5_N = 64
S5_T = 8
S5_GPT = LANES // S5_P
S5_HALF = S5_GPT * S5_N
S5_STATE = 2 * S5_HALF
S5_ROW = S5_T * LANES


def _layer_norm(y, g, b, eps=LN_EPS):
    mu = jnp.mean(y, axis=-1, keepdims=True)
    yc = y - mu
    var = jnp.mean(yc * yc, axis=-1, keepdims=True)
    return yc * lax.rsqrt(var + eps) * g + b


def _rms_norm(y, g):
    return y * lax.rsqrt(jnp.mean(y * y, axis=-1, keepdims=True) + RMS_EPS) * g


def _sigmoid(x):
    return 1.0 / (1.0 + jnp.exp(-x))


def _gelu_tanh(x):
    c = 0.7978845608028654
    return 0.5 * x * (1.0 + jnp.tanh(c * (x + 0.044715 * (x * x * x))))


def _cmul(ar, ai, br, bi):
    return ar * br - ai * bi, ar * bi + ai * br


def _params(sem, est_bytes):
    limit = min(VMEM_CAP_BYTES, est_bytes + COMPILER_TEMP_BYTES)
    return pltpu.CompilerParams(dimension_semantics=sem, vmem_limit_bytes=limit)


def _ffn_kernel(x_ref, wg_ref, wu_ref, wd_ref, g_ref, b_ref, *rest, alpha, cast_next):
    if cast_next:
        ng_ref, nu_ref, nd_ref, o_ref, cg_ref, cu_ref, cd_ref, xb_ref = rest
    else:
        o_ref, xb_ref = rest
    f = pl.program_id(1)
    last = pl.num_programs(1) - 1

    def weights():
        if cast_next:
            cg_ref[...] = ng_ref[...].astype(BF16)
            cu_ref[...] = nu_ref[...].astype(BF16)
            cd_ref[...] = nd_ref[...].astype(BF16)
        return wg_ref[...].astype(BF16), wu_ref[...].astype(BF16), wd_ref[...].astype(BF16)

    def swiglu_down(xb, wg, wu, wd):
        gate = jnp.dot(xb, wg, preferred_element_type=F32)
        up = jnp.dot(xb, wu, preferred_element_type=F32)
        h = (gate * _sigmoid(gate)) * up
        return jnp.dot(h.astype(BF16), wd, preferred_element_type=F32)

    @pl.when(f == 0)
    def _():
        x = x_ref[...]
        xb = x.astype(BF16)
        xb_ref[...] = xb
        o_ref[...] = (alpha / FFN_BETA) * x + swiglu_down(xb, *weights())

    @pl.when(jnp.logical_and(f > 0, f < last))
    def _():
        o_ref[...] += swiglu_down(xb_ref[...], *weights())

    @pl.when(f == last)
    def _():
        acc = o_ref[...] + swiglu_down(xb_ref[...], *weights())
        o_ref[...] = _layer_norm(acc, g_ref[...], b_ref[...], LN_EPS / (FFN_BETA * FFN_BETA))


def _ffn(x, weights, wlayer, ln_g, ln_b, layer, alpha, next_weights=None, next_layer=0, tm=1024):
    wg, wu, wd = weights
    m, d = x.shape
    ff = wg.shape[-1]
    wbytes = wg.dtype.itemsize
    tf = 2 * LANES * (4 // wbytes)
    ni, nf = m // tm, ff // tf
    cast_next = next_weights is not None
    dr = d // ni
    assert ff % tf == 0 and nf >= 2 and m % tm == 0 and d % (ni * LANES) == 0
    est = (2 * 2 * tm * d * 4 + 2 * 3 * d * tf * wbytes + tm * d * 2 + (wbytes // 4) * 3 * d * tf * 2
           + 2 * tm * tf * 4 + tm * tf * 2 + cast_next * 2 * 3 * dr * tf * 6)
    in_specs = [
        pl.BlockSpec((tm, d), lambda i, f: (i, 0)),
        pl.BlockSpec((None, d, tf), lambda i, f: (wlayer, 0, f)),
        pl.BlockSpec((None, d, tf), lambda i, f: (wlayer, 0, f)),
        pl.BlockSpec((None, tf, d), lambda i, f: (wlayer, f, 0)),
        pl.BlockSpec((None, 1, d), lambda i, f: (layer, 0, 0)),
        pl.BlockSpec((None, 1, d), lambda i, f: (layer, 0, 0)),
    ]
    out_specs = [pl.BlockSpec((tm, d), lambda i, f: (i, 0))]
    out_shape = [jax.ShapeDtypeStruct((m, d), F32)]
    args = [x, wg, wu, wd, ln_g, ln_b]
    if cast_next:
        in_specs += [
            pl.BlockSpec((None, dr, tf), lambda i, f: (next_layer, i, f)),
            pl.BlockSpec((None, dr, tf), lambda i, f: (next_layer, i, f)),
            pl.BlockSpec((None, tf, dr), lambda i, f: (next_layer, f, i)),
        ]
        out_specs += [
            pl.BlockSpec((None, dr, tf), lambda i, f: (0, i, f)),
            pl.BlockSpec((None, dr, tf), lambda i, f: (0, i, f)),
            pl.BlockSpec((None, tf, dr), lambda i, f: (0, f, i)),
        ]
        out_shape += [jax.ShapeDtypeStruct((1,) + w.shape[1:], BF16) for w in next_weights]
        args += list(next_weights)
    outs = pl.pallas_call(
        functools.partial(_ffn_kernel, alpha=alpha, cast_next=cast_next),
        grid=(ni, nf),
        in_specs=in_specs,
        out_specs=out_specs,
        out_shape=out_shape,
        scratch_shapes=[pltpu.VMEM((tm, d), BF16)],
        compiler_params=_params(("arbitrary", "arbitrary"), est),
        name="ffn_cast" if cast_next else "ffn",
    )(*args)
    return outs[0], tuple(outs[1:])


def _proj_kernel(x_ref, win_ref, cw_ref, cb_ref, gc_ref, nglu_ref, nout_ref,
                 u_ref, z_ref, cglu_ref, cout_ref, us_ref, vpad_ref, *, tiles_per_seq):
    tm = x_ref.shape[0]
    nt = u_ref.shape[0]
    d_s5 = nt * LANES
    d_conv = z_ref.shape[1]

    @pl.when(pl.program_id(0) % tiles_per_seq == 0)
    def _():
        vpad_ref[0:SUBLANES, :] = jnp.zeros((SUBLANES, d_conv), F32)

    cglu_ref[...] = nglu_ref[...].astype(BF16)
    cout_ref[...] = nout_ref[...].astype(BF16)

    xb = x_ref[...].astype(BF16)

    u = jnp.dot(xb, win_ref[:, 0:d_s5], preferred_element_type=F32)
    for j in range(nt):
        us_ref[j] = u[:, j * LANES:(j + 1) * LANES]
    for j in range(nt):
        for t in range(S5_T):
            rows = us_ref[j, pl.ds(t, tm // S5_T, stride=S5_T), :]
            u_ref[j, :, t * LANES:(t + 1) * LANES] = rows.astype(BF16)

    o = d_s5
    gate_c = jnp.dot(xb, win_ref[:, o + d_conv:o + 2 * d_conv], preferred_element_type=F32)
    hid = jnp.dot(xb, win_ref[:, o + 2 * d_conv:o + 3 * d_conv], preferred_element_type=F32)
    v = gate_c * hid
    vpad_ref[SUBLANES:SUBLANES + tm, :] = v
    v1 = vpad_ref[SUBLANES - 1:SUBLANES - 1 + tm, :]
    v2 = vpad_ref[SUBLANES - 2:SUBLANES - 2 + tm, :]
    conv = cb_ref[...] + cw_ref[0:1, :] * v2 + cw_ref[1:2, :] * v1 + cw_ref[2:3, :] * v
    vpad_ref[0:SUBLANES, :] = vpad_ref[tm:tm + SUBLANES, :]

    gate_b = jnp.dot(xb, win_ref[:, o:o + d_conv], preferred_element_type=F32)
    z_ref[...] = _rms_norm(gate_b * conv, gc_ref[...]).astype(BF16)


def _proj(x, w_in, conv_w, conv_b, g_conv, w_glu, w_out, layer, seq, d_s5, tm=512):
    m, d = x.shape
    d_in = w_in.shape[-1]
    d_conv = conv_b.shape[-1]
    nt = d_s5 // LANES
    steps = m // tm
    rg, ro = w_glu.shape[1] // steps, w_out.shape[1] // steps
    assert CONV_W - 1 <= SUBLANES and conv_w.shape[1] == CONV_W and seq % tm == 0
    assert tm % (S5_T * 2 * SUBLANES) == 0
    assert w_glu.shape[1] % (steps * 2 * SUBLANES) == 0 and w_out.shape[1] % (steps * 2 * SUBLANES) == 0
    est = (2 * tm * d * 4 + d * d_in * 2 + 2 * tm * (d_s5 + d_conv) * 2
           + tm * d_s5 * 4 + (tm + SUBLANES) * d_conv * 4 + tm * d * 2 + 6 * tm * d_conv * 4
           + 2 * (rg * w_glu.shape[2] + ro * w_out.shape[2]) * 6)
    return pl.pallas_call(
        functools.partial(_proj_kernel, tiles_per_seq=seq // tm),
        grid=(steps,),
        in_specs=[
            pl.BlockSpec((tm, d), lambda i: (i, 0)),
            pl.BlockSpec((None, d, d_in), lambda i: (0, 0, 0), pipeline_mode=pl.Buffered(1)),
            pl.BlockSpec((None, CONV_W, d_conv), lambda i: (layer, 0, 0)),
            pl.BlockSpec((None, 1, d_conv), lambda i: (layer, 0, 0)),
            pl.BlockSpec((None, 1, d_conv), lambda i: (layer, 0, 0)),
            pl.BlockSpec((None, rg, w_glu.shape[2]), lambda i: (layer, i, 0)),
            pl.BlockSpec((None, ro, w_out.shape[2]), lambda i: (layer, i, 0)),
        ],
        out_specs=[
            pl.BlockSpec((nt, tm // S5_T, S5_ROW), lambda i: (0, i, 0)),
            pl.BlockSpec((tm, d_conv), lambda i: (i, 0)),
            pl.BlockSpec((None, rg, w_glu.shape[2]), lambda i: (0, i, 0)),
            pl.BlockSpec((None, ro, w_out.shape[2]), lambda i: (0, i, 0)),
        ],
        out_shape=[
            jax.ShapeDtypeStruct((nt, m // S5_T, S5_ROW), BF16),
            jax.ShapeDtypeStruct((m, d_conv), BF16),
            jax.ShapeDtypeStruct((1,) + w_glu.shape[1:], BF16),
            jax.ShapeDtypeStruct((1,) + w_out.shape[1:], BF16),
        ],
        scratch_shapes=[pltpu.VMEM((nt, tm, LANES), F32), pltpu.VMEM((tm + SUBLANES, d_conv), F32)],
        compiler_params=_params(("arbitrary",), est),
        name="proj_conv",
    )(x, w_in, conv_w, conv_b, g_conv, w_glu, w_out)


def _discretize(lre, lim, ldt):
    dt = jnp.exp(ldt)
    mag = jnp.exp(lre * dt)
    ang = lim * dt
    return mag * jnp.cos(ang), mag * jnp.sin(ang)


def _s5_build_operators(bre_ref, bim_ref, cre_ref, cim_ref, lre_ref, lim_ref, ldt_ref,
                        slre_ref, slim_ref, sldt_ref, wt_ref, wl_ref, ws_ref, at_ref):
    lre, lim = lre_ref[...], lim_ref[...]
    ab_re, ab_im = _discretize(lre, lim, ldt_ref[...])
    den = lre * lre + lim * lim
    nr, ni = ab_re - 1.0, ab_im
    q_re = (nr * lre + ni * lim) / den
    q_im = (ni * lre - nr * lim) / den
    b_re, b_im = bre_ref[...], bim_ref[...]
    bb_re = q_re * b_re - q_im * b_im
    bb_im = q_re * b_im + q_im * b_re
    c_re, c_im = cre_ref[...], cim_ref[...]

    rows = lax.broadcasted_iota(jnp.int32, (S5_HALF, LANES), 0) // S5_N
    cols = lax.broadcasted_iota(jnp.int32, (S5_HALF, LANES), 1) // S5_P
    same_group = rows == cols
    same_group_sq = (lax.broadcasted_iota(jnp.int32, (LANES, LANES), 0) // S5_P
                     == lax.broadcasted_iota(jnp.int32, (LANES, LANES), 1) // S5_P)

    def blk(a):
        return jnp.where(same_group, jnp.concatenate([a] * S5_GPT, axis=0), 0.0)

    def dot_exact(a, b):
        return jnp.dot(a, b, precision=lax.Precision.HIGHEST, preferred_element_type=F32)

    p_re, p_im = jnp.ones_like(ab_re), jnp.zeros_like(ab_re)
    toeplitz = []
    for k in range(S5_T + 1):
        if k < S5_T:
            bl_re, bl_im = _cmul(p_re, p_im, bb_re, bb_im)
            tau = S5_T - 1 - k
            wl_ref[tau * LANES:(tau + 1) * LANES, 0:S5_HALF] = blk(bl_re).T.astype(BF16)
            wl_ref[tau * LANES:(tau + 1) * LANES, S5_HALF:S5_STATE] = blk(bl_im).T.astype(BF16)
            kk = dot_exact(bl_re.T, c_re) - dot_exact(bl_im.T, c_im)
            toeplitz.append(jnp.where(same_group_sq, kk, 0.0).astype(BF16))
        if k >= 1:
            cl_re, cl_im = _cmul(p_re, p_im, c_re, c_im)
            t = k - 1
            ws_ref[0:S5_HALF, t * LANES:(t + 1) * LANES] = blk(cl_re).astype(BF16)
            ws_ref[S5_HALF:S5_STATE, t * LANES:(t + 1) * LANES] = (-blk(cl_im)).astype(BF16)
        p_re, p_im = _cmul(p_re, p_im, ab_re, ab_im)

    zero = jnp.zeros((LANES, LANES), BF16)
    for tau in range(S5_T):
        for t in range(S5_T):
            wt_ref[tau * LANES:(tau + 1) * LANES, t * LANES:(t + 1) * LANES] = (
                toeplitz[t - tau] if t >= tau else zero)

    a_re, a_im = _discretize(slre_ref[...], slim_ref[...], sldt_ref[...])
    step = 1
    while step < S5_T:
        a_re, a_im = _cmul(a_re, a_im, a_re, a_im)
        step *= 2
    at_ref[0:1, :] = a_re
    at_ref[1:2, :] = a_im


def _s5_param_layouts(lam_re, lam_im, log_dt, b_re, b_im, c_re, c_im):
    depth, g, n = lam_re.shape
    p = b_re.shape[-1]
    assert (p, n) == (S5_P, S5_N) and S5_T & (S5_T - 1) == 0
    nt = g // S5_GPT

    def b_layout(a):
        return a.reshape(depth, nt, S5_GPT, n, p).transpose(0, 1, 3, 2, 4).reshape(depth, nt, n, LANES)

    def c_layout(a):
        return a.reshape(depth, nt, S5_GPT, p, n).transpose(0, 1, 4, 2, 3).reshape(depth, nt, n, LANES)

    def lam_layout(a):
        a = a.reshape(depth, nt, S5_GPT, n).transpose(0, 1, 3, 2)
        return jnp.broadcast_to(a[..., None], (depth, nt, n, S5_GPT, p)).reshape(depth, nt, n, LANES)

    ldt = jnp.broadcast_to(log_dt.reshape(depth, nt, 1, S5_GPT, 1), (depth, nt, n, S5_GPT, p))
    ldt = ldt.reshape(depth, nt, n, LANES)
    s_lre = lam_re.reshape(depth, nt, 1, S5_HALF)
    s_lim = lam_im.reshape(depth, nt, 1, S5_HALF)
    s_ldt = jnp.broadcast_to(log_dt.reshape(depth, nt, S5_GPT, 1), (depth, nt, S5_GPT, n))
    s_ldt = s_ldt.reshape(depth, nt, 1, S5_HALF)

    return (b_layout(b_re), b_layout(b_im), c_layout(c_re), c_layout(c_im),
            lam_layout(lam_re), lam_layout(lam_im), ldt, s_lre, s_lim, s_ldt)


def _s5_kernel(u_ref, *rest, batch, chunks, pitch):
    param_refs, (d_ref, y_ref, wt_ref, wl_ref, ws_ref, at_ref, loc_ref, st_ref) = rest[:10], rest[10:]
    _s5_build_operators(*param_refs, wt_ref, wl_ref, ws_ref, at_ref)
    x = u_ref[...]
    loc = jnp.dot(x, wl_ref[...], preferred_element_type=F32)
    slabs = S5_STATE // LANES
    half = slabs // 2
    for k in range(slabs):
        for b in range(batch):
            loc_ref[k, b * pitch:b * pitch + chunks, :] = loc[b * chunks:(b + 1) * chunks, k * LANES:(k + 1) * LANES]
    a_re = [at_ref[0:1, k * LANES:(k + 1) * LANES] for k in range(half)]
    a_im = [at_ref[1:2, k * LANES:(k + 1) * LANES] for k in range(half)]

    def body(c, carry):
        rows = pl.ds(c, batch, stride=pitch)
        new = [None] * slabs
        for k in range(half):
            s_re, s_im = carry[k], carry[half + k]
            st_ref[k, rows, :] = s_re
            st_ref[half + k, rows, :] = s_im
            new[k] = a_re[k] * s_re - a_im[k] * s_im + loc_ref[k, rows, :]
            new[half + k] = a_re[k] * s_im + a_im[k] * s_re + loc_ref[half + k, rows, :]
        return tuple(new)

    carry = (jnp.zeros((batch, LANES), F32),) * slabs
    for c in range(chunks):
        carry = body(c, carry)

    pair = 2 * LANES
    y = jnp.concatenate(
        [jnp.dot(x[:, 0:(c + 1) * pair], wt_ref[0:(c + 1) * pair, c * pair:(c + 1) * pair],
                 preferred_element_type=F32) for c in range(S5_ROW // pair)], axis=-1)
    st = jnp.concatenate(
        [jnp.concatenate([st_ref[k, b * pitch:b * pitch + chunks, :] for b in range(batch)], axis=0)
         for k in range(slabs)], axis=-1).astype(BF16)
    y = y + jnp.dot(st, ws_ref[...], preferred_element_type=F32)
    y_ref[...] = (y + d_ref[...] * x.astype(F32)).astype(BF16)


def _s5(u2, params, d_rows, layer, batch, seq):
    nt, rows, _ = u2.shape
    chunks = seq // S5_T
    assert rows == batch * chunks and chunks % SUBLANES == 0
    pitch = chunks + SUBLANES if (chunks // SUBLANES) % 2 == 0 else chunks
    est = (2 * 2 * rows * S5_ROW * 2 + 2 * (S5_ROW * S5_ROW + 2 * S5_ROW * S5_STATE)
           + 2 * batch * pitch * S5_STATE * 4 + 3 * rows * S5_ROW * 4 + rows * S5_STATE * 2
           + 16 * S5_HALF * LANES * 4)
    tile_spec = pl.BlockSpec((None, None, S5_N, LANES), lambda j: (layer, j, 0, 0))
    lane_spec = pl.BlockSpec((None, None, 1, S5_HALF), lambda j: (layer, j, 0, 0))
    return pl.pallas_call(
        functools.partial(_s5_kernel, batch=batch, chunks=chunks, pitch=pitch),
        grid=(nt,),
        in_specs=[pl.BlockSpec((None, rows, S5_ROW), lambda j: (j, 0, 0))] + [tile_spec] * 7 + [lane_spec] * 3
        + [pl.BlockSpec((None, None, 1, S5_ROW), lambda j: (layer, j, 0, 0))],
        out_specs=pl.BlockSpec((None, rows, S5_ROW), lambda j: (j, 0, 0)),
        out_shape=jax.ShapeDtypeStruct((nt, rows, S5_ROW), BF16),
        scratch_shapes=[
            pltpu.VMEM((S5_ROW, S5_ROW), BF16),
            pltpu.VMEM((S5_ROW, S5_STATE), BF16),
            pltpu.VMEM((S5_STATE, S5_ROW), BF16),
            pltpu.VMEM((2, S5_HALF), F32),
            pltpu.VMEM((S5_STATE // LANES, batch * pitch, LANES), F32),
            pltpu.VMEM((S5_STATE // LANES, batch * pitch, LANES), F32),
        ],
        compiler_params=_params(("parallel",), est),
        name="s5_mix",
    )(u2, *params, d_rows)


def _out_kernel(y_ref, z_ref, x_ref, wglu_ref, wout_ref, gs_ref, g_ref, b_ref, *rest, alpha, cast_next):
    if cast_next:
        nin_ref, o_ref, cin_ref, ys_ref = rest
        cin_ref[...] = nin_ref[...].astype(BF16)
    else:
        o_ref, ys_ref = rest
    nt = y_ref.shape[0]
    tm = x_ref.shape[0]
    for j in range(nt):
        for t in range(S5_T):
            ys_ref[j, pl.ds(t, tm // S5_T, stride=S5_T), :] = y_ref[j, :, t * LANES:(t + 1) * LANES].astype(F32)
    y = jnp.concatenate([ys_ref[j] for j in range(nt)], axis=-1)
    y = _gelu_tanh(y)
    y = y * _sigmoid(jnp.dot(y.astype(BF16), wglu_ref[...], preferred_element_type=F32))
    y = _rms_norm(y, gs_ref[...])
    yz = jnp.concatenate([y.astype(BF16), z_ref[...]], axis=-1)
    mix = jnp.dot(yz, wout_ref[...], preferred_element_type=F32)
    o_ref[...] = _layer_norm(alpha * x_ref[...] + mix, g_ref[...], b_ref[...])


def _out(y2, z, x, w_glu, w_out, g_s5, ln_g, ln_b, layer, alpha, next_w_in=None, next_layer=0, tm=512):
    m, d = x.shape
    nt = y2.shape[0]
    d_s5 = nt * LANES
    d_conv = z.shape[1]
    steps = m // tm
    cast_next = next_w_in is not None
    assert tm % (S5_T * 2 * SUBLANES) == 0
    est = (2 * 2 * tm * d * 4 + 2 * tm * (d_s5 + d_conv) * 2
           + (d_s5 * d_s5 + (d_s5 + d_conv) * d) * 2
           + tm * d_s5 * 4 + 3 * tm * d_s5 * 4 + tm * d * 4)
    in_specs = [
        pl.BlockSpec((nt, tm // S5_T, S5_ROW), lambda i: (0, i, 0)),
        pl.BlockSpec((tm, d_conv), lambda i: (i, 0)),
        pl.BlockSpec((tm, d), lambda i: (i, 0)),
        pl.BlockSpec((None, d_s5, d_s5), lambda i: (0, 0, 0), pipeline_mode=pl.Buffered(1)),
        pl.BlockSpec((None, d_s5 + d_conv, d), lambda i: (0, 0, 0), pipeline_mode=pl.Buffered(1)),
        pl.BlockSpec((None, 1, d_s5), lambda i: (layer, 0, 0)),
        pl.BlockSpec((None, 1, d), lambda i: (layer, 0, 0)),
        pl.BlockSpec((None, 1, d), lambda i: (layer, 0, 0)),
    ]
    out_specs = [pl.BlockSpec((tm, d), lambda i: (i, 0))]
    out_shape = [jax.ShapeDtypeStruct((m, d), F32)]
    args = [y2, z, x, w_glu, w_out, g_s5, ln_g, ln_b]
    if cast_next:
        rows, cols = next_w_in.shape[1] // steps, next_w_in.shape[2]
        assert next_w_in.shape[1] % (steps * 2 * SUBLANES) == 0
        in_specs.append(pl.BlockSpec((None, rows, cols), lambda i: (next_layer, i, 0)))
        out_specs.append(pl.BlockSpec((None, rows, cols), lambda i: (0, i, 0)))
        out_shape.append(jax.ShapeDtypeStruct((1,) + next_w_in.shape[1:], BF16))
        args.append(next_w_in)
        est += 2 * rows * cols * 6
    outs = pl.pallas_call(
        functools.partial(_out_kernel, alpha=alpha, cast_next=cast_next),
        grid=(steps,),
        in_specs=in_specs,
        out_specs=out_specs,
        out_shape=out_shape,
        scratch_shapes=[pltpu.VMEM((nt, tm, LANES), F32)],
        compiler_params=_params(("arbitrary",), est),
        name="glu_out_cast" if cast_next else "glu_out",
    )(*args)
    return outs[0], (outs[1] if cast_next else None)


def kernel(x, ffn1_gate, ffn1_up, ffn1_down, ln1_g, ln1_b, w_in, s5_lam_re, s5_lam_im, s5_log_dt,
           s5_b_re, s5_b_im, s5_c_re, s5_c_im, s5_d, s5_w_glu, conv_w, conv_b, g_s5, g_conv, w_out,
           ln2_g, ln2_b, ffn2_gate, ffn2_up, ffn2_down, ln3_g, ln3_b):
    batch, seq, d_model = x.shape
    depth = ffn1_gate.shape[0]
    d_s5 = s5_w_glu.shape[-1]
    alpha = (2.0 * depth) ** 0.25
    assert d_s5 % LANES == 0 and seq % S5_T == 0

    def row(a):
        return a.reshape(depth, 1, a.shape[-1])

    ffns = []
    for l in range(depth):
        ffns.append(((ffn1_gate, ffn1_up, ffn1_down), row(ln1_g), row(ln1_b), l))
        ffns.append(((ffn2_gate, ffn2_up, ffn2_down), row(ln3_g), row(ln3_b), l))
    w_in_b = w_in[0:1].astype(BF16)

    s5_params = _s5_param_layouts(s5_lam_re, s5_lam_im, s5_log_dt, s5_b_re, s5_b_im, s5_c_re, s5_c_im)
    nt = d_s5 // LANES
    d_rows = jnp.tile(s5_d.reshape(depth, nt, 1, LANES), (1, 1, 1, S5_T))

    state = {"h": x.reshape(batch * seq, d_model), "weights": ffns[0][0], "wlayer": 0, "k": 0}

    def ffn():
        k = state["k"]
        _, g, b, l = ffns[k]
        nxt = ffns[k + 1] if k + 1 < len(ffns) else None
        state["h"], cast = _ffn(state["h"], state["weights"], state["wlayer"], g, b, l, alpha,
                                nxt[0] if nxt else None, nxt[3] if nxt else 0)
        state["weights"], state["wlayer"], state["k"] = cast, 0, k + 1

    for l in range(depth):
        ffn()
        h = state["h"]
        u2, z, w_glu_b, w_out_b = _proj(h, w_in_b, conv_w, row(conv_b), row(g_conv), s5_w_glu, w_out,
                                        l, seq, d_s5)
        y2 = _s5(u2, s5_params, d_rows, l, batch, seq)
        state["h"], w_in_b = _out(y2, z, h, w_glu_b, w_out_b, row(g_s5), row(ln2_g), row(ln2_b), l, alpha,
                                  w_in if l + 1 < depth else None, l + 1)
        ffn()
    return state["h"].reshape(batch, seq, d_model)
```

```python
import functools

import jax
import jax.numpy as jnp
from jax import lax
from jax.experimental import pallas as pl
from jax.experimental.pallas import tpu as pltpu

F32 = jnp.float32
BF16 = jnp.bfloat16

LN_EPS = 1e-5
RMS_EPS = 1e-6
CONV_W = 3
FFN_BETA = 0.5

LANES = 128
SUBLANES = 8
MXU_COLS = 256
V7X_VMEM_BYTES = 64 * 1024 * 1024
VMEM_CAP_BYTES = V7X_VMEM_BYTES - 2 * 1024 * 1024
COMPILER_TEMP_BYTES = 8 * 1024 * 1024

S5_P = 16
S5_N = 64
S5_T = 8
S5_GPT = LANES // S5_P
S5_HALF = S5_GPT * S5_N
S5_STATE = 2 * S5_HALF
S5_ROW = S5_T * LANES


def _layer_norm(y, g, b, eps=LN_EPS):
    mu = jnp.mean(y, axis=-1, keepdims=True)
    yc = y - mu
    var = jnp.mean(yc * yc, axis=-1, keepdims=True)
    return yc * lax.rsqrt(var + eps) * g + b


def _rms_norm(y, g):
    return y * lax.rsqrt(jnp.mean(y * y, axis=-1, keepdims=True) + RMS_EPS) * g


def _sigmoid(x):
    return 1.0 / (1.0 + jnp.exp(-x))


def _gelu_tanh(x):
    c = 0.7978845608028654
    return 0.5 * x * (1.0 + jnp.tanh(c * (x + 0.044715 * (x * x * x))))


def _cmul(ar, ai, br, bi):
    return ar * br - ai * bi, ar * bi + ai * br


def _params(sem, est_bytes):
    limit = min(VMEM_CAP_BYTES, est_bytes + COMPILER_TEMP_BYTES)
    return pltpu.CompilerParams(dimension_semantics=sem, vmem_limit_bytes=limit)


def _ffn_kernel(x_ref, wg_ref, wu_ref, wd_ref, g_ref, b_ref, *rest, alpha, cast_next):
    if cast_next:
        ng_ref, nu_ref, nd_ref, o_ref, cg_ref, cu_ref, cd_ref, xb_ref = rest
    else:
        o_ref, xb_ref = rest
    f = pl.program_id(1)
    last = pl.num_programs(1) - 1

    def weights():
        if cast_next:
            cg_ref[...] = ng_ref[...].astype(BF16)
            cu_ref[...] = nu_ref[...].astype(BF16)
            cd_ref[...] = nd_ref[...].astype(BF16)
        return wg_ref[...].astype(BF16), wu_ref[...].astype(BF16), wd_ref[...].astype(BF16)

    def swiglu_down(xb, wg, wu, wd):
        hs = []
        for c0 in range(0, wg.shape[1], MXU_COLS):
            gate = jnp.dot(xb, wg[:, c0:c0 + MXU_COLS], preferred_element_type=F32)
            up = jnp.dot(xb, wu[:, c0:c0 + MXU_COLS], preferred_element_type=F32)
            hs.append(((gate * _sigmoid(gate)) * up).astype(BF16))
        h = hs[0] if len(hs) == 1 else jnp.concatenate(hs, axis=-1)
        return jnp.dot(h, wd, preferred_element_type=F32)

    @pl.when(f == 0)
    def _():
        x = x_ref[...]
        xb = x.astype(BF16)
        xb_ref[...] = xb
        o_ref[...] = (alpha / FFN_BETA) * x + swiglu_down(xb, *weights())

    @pl.when(jnp.logical_and(f > 0, f < last))
    def _():
        o_ref[...] += swiglu_down(xb_ref[...], *weights())

    @pl.when(f == last)
    def _():
        acc = o_ref[...] + swiglu_down(xb_ref[...], *weights())
        o_ref[...] = _layer_norm(acc, g_ref[...], b_ref[...], LN_EPS / (FFN_BETA * FFN_BETA))


def _ffn(x, weights, wlayer, ln_g, ln_b, layer, alpha, next_weights=None, next_layer=0, tm=1024):
    wg, wu, wd = weights
    m, d = x.shape
    ff = wg.shape[-1]
    wbytes = wg.dtype.itemsize
    tf = 2 * LANES * (4 // wbytes)
    ni, nf = m // tm, ff // tf
    cast_next = next_weights is not None
    dr = d // ni
    assert ff % tf == 0 and nf >= 2 and m % tm == 0 and d % (ni * LANES) == 0
    est = (2 * 2 * tm * d * 4 + 2 * 3 * d * tf * wbytes + tm * d * 2 + (wbytes // 4) * 3 * d * tf * 2
           + 2 * tm * tf * 4 + tm * tf * 2 + cast_next * 2 * 3 * dr * tf * 6)
    in_specs = [
        pl.BlockSpec((tm, d), lambda i, f: (i, 0)),
        pl.BlockSpec((None, d, tf), lambda i, f: (wlayer, 0, f)),
        pl.BlockSpec((None, d, tf), lambda i, f: (wlayer, 0, f)),
        pl.BlockSpec((None, tf, d), lambda i, f: (wlayer, f, 0)),
        pl.BlockSpec((None, 1, d), lambda i, f: (layer, 0, 0)),
        pl.BlockSpec((None, 1, d), lambda i, f: (layer, 0, 0)),
    ]
    out_specs = [pl.BlockSpec((tm, d), lambda i, f: (i, 0))]
    out_shape = [jax.ShapeDtypeStruct((m, d), F32)]
    args = [x, wg, wu, wd, ln_g, ln_b]
    if cast_next:
        in_specs += [
            pl.BlockSpec((None, dr, tf), lambda i, f: (next_layer, i, f)),
            pl.BlockSpec((None, dr, tf), lambda i, f: (next_layer, i, f)),
            pl.BlockSpec((None, tf, dr), lambda i, f: (next_layer, f, i)),
        ]
        out_specs += [
            pl.BlockSpec((None, dr, tf), lambda i, f: (0, i, f)),
            pl.BlockSpec((None, dr, tf), lambda i, f: (0, i, f)),
            pl.BlockSpec((None, tf, dr), lambda i, f: (0, f, i)),
        ]
        out_shape += [jax.ShapeDtypeStruct((1,) + w.shape[1:], BF16) for w in next_weights]
        args += list(next_weights)
    outs = pl.pallas_call(
        functools.partial(_ffn_kernel, alpha=alpha, cast_next=cast_next),
        grid=(ni, nf),
        in_specs=in_specs,
        out_specs=out_specs,
        out_shape=out_shape,
        scratch_shapes=[pltpu.VMEM((tm, d), BF16)],
        compiler_params=_params(("arbitrary", "arbitrary"), est),
        name="ffn_cast" if cast_next else "ffn",
    )(*args)
    return outs[0], tuple(outs[1:])


def _proj_kernel(x_ref, win_ref, cw_ref, cb_ref, gc_ref, nglu_ref, nout_ref,
                 u_ref, z_ref, cglu_ref, cout_ref, us_ref, vpad_ref, *, tiles_per_seq):
    tm = x_ref.shape[0]
    nt = u_ref.shape[0]
    d_s5 = nt * LANES
    d_conv = z_ref.shape[1]

    @pl.when(pl.program_id(0) % tiles_per_seq == 0)
    def _():
        vpad_ref[0:SUBLANES, :] = jnp.zeros((SUBLANES, d_conv), F32)

    cglu_ref[...] = nglu_ref[...].astype(BF16)
    cout_ref[...] = nout_ref[...].astype(BF16)

    xb = x_ref[...].astype(BF16)

    u = jnp.dot(xb, win_ref[:, 0:d_s5], preferred_element_type=F32)
    for j in range(nt):
        us_ref[j] = u[:, j * LANES:(j + 1) * LANES]
    for j in range(nt):
        for t in range(S5_T):
            rows = us_ref[j, pl.ds(t, tm // S5_T, stride=S5_T), :]
            u_ref[j, :, t * LANES:(t + 1) * LANES] = rows.astype(BF16)

    o = d_s5
    gate_c = jnp.dot(xb, win_ref[:, o + d_conv:o + 2 * d_conv], preferred_element_type=F32)
    hid = jnp.dot(xb, win_ref[:, o + 2 * d_conv:o + 3 * d_conv], preferred_element_type=F32)
    v = gate_c * hid
    vpad_ref[SUBLANES:SUBLANES + tm, :] = v
    v1 = vpad_ref[SUBLANES - 1:SUBLANES - 1 + tm, :]
    v2 = vpad_ref[SUBLANES - 2:SUBLANES - 2 + tm, :]
    conv = cb_ref[...] + cw_ref[0:1, :] * v2 + cw_ref[1:2, :] * v1 + cw_ref[2:3, :] * v
    vpad_ref[0:SUBLANES, :] = vpad_ref[tm:tm + SUBLANES, :]

    gate_b = jnp.dot(xb, win_ref[:, o:o + d_conv], preferred_element_type=F32)
    z_ref[...] = _rms_norm(gate_b * conv, gc_ref[...]).astype(BF16)


def _proj(x, w_in, conv_w, conv_b, g_conv, w_glu, w_out, layer, seq, d_s5, tm=512):
    m, d = x.shape
    d_in = w_in.shape[-1]
    d_conv = conv_b.shape[-1]
    nt = d_s5 // LANES
    steps = m // tm
    rg, ro = w_glu.shape[1] // steps, w_out.shape[1] // steps
    assert CONV_W - 1 <= SUBLANES and conv_w.shape[1] == CONV_W and seq % tm == 0
    assert tm % (S5_T * 2 * SUBLANES) == 0
    assert w_glu.shape[1] % (steps * 2 * SUBLANES) == 0 and w_out.shape[1] % (steps * 2 * SUBLANES) == 0
    est = (2 * tm * d * 4 + d * d_in * 2 + 2 * tm * (d_s5 + d_conv) * 2
           + tm * d_s5 * 4 + (tm + SUBLANES) * d_conv * 4 + tm * d * 2 + 6 * tm * d_conv * 4
           + 2 * (rg * w_glu.shape[2] + ro * w_out.shape[2]) * 6)
    return pl.pallas_call(
        functools.partial(_proj_kernel, tiles_per_seq=seq // tm),
        grid=(steps,),
        in_specs=[
            pl.BlockSpec((tm, d), lambda i: (i, 0)),
            pl.BlockSpec((None, d, d_in), lambda i: (0, 0, 0), pipeline_mode=pl.Buffered(1)),
            pl.BlockSpec((None, CONV_W, d_conv), lambda i: (layer, 0, 0)),
            pl.BlockSpec((None, 1, d_conv), lambda i: (layer, 0, 0)),
            pl.BlockSpec((None, 1, d_conv), lambda i: (layer, 0, 0)),
            pl.BlockSpec((None, rg, w_glu.shape[2]), lambda i: (layer, i, 0)),
            pl.BlockSpec((None, ro, w_out.shape[2]), lambda i: (layer, i, 0)),
        ],
        out_specs=[
            pl.BlockSpec((nt, tm // S5_T, S5_ROW), lambda i: (0, i, 0)),
            pl.BlockSpec((tm, d_conv), lambda i: (i, 0)),
            pl.BlockSpec((None, rg, w_glu.shape[2]), lambda i: (0, i, 0)),
            pl.BlockSpec((None, ro, w_out.shape[2]), lambda i: (0, i, 0)),
        ],
        out_shape=[
            jax.ShapeDtypeStruct((nt, m // S5_T, S5_ROW), BF16),
            jax.ShapeDtypeStruct((m, d_conv), BF16),
            jax.ShapeDtypeStruct((1,) + w_glu.shape[1:], BF16),
            jax.ShapeDtypeStruct((1,) + w_out.shape[1:], BF16),
        ],
        scratch_shapes=[pltpu.VMEM((nt, tm, LANES), F32), pltpu.VMEM((tm + SUBLANES, d_conv), F32)],
        compiler_params=_params(("arbitrary",), est),
        name="proj_conv",
    )(x, w_in, conv_w, conv_b, g_conv, w_glu, w_out)


def _discretize(lre, lim, ldt):
    dt = jnp.exp(ldt)
    mag = jnp.exp(lre * dt)
    ang = lim * dt
    return mag * jnp.cos(ang), mag * jnp.sin(ang)


def _s5_build_operators(bre_ref, bim_ref, cre_ref, cim_ref, lre_ref, lim_ref, ldt_ref,
                        slre_ref, slim_ref, sldt_ref, wt_ref, wl_ref, ws_ref, at_ref):
    lre, lim = lre_ref[...], lim_ref[...]
    ab_re, ab_im = _discretize(lre, lim, ldt_ref[...])
    den = lre * lre + lim * lim
    nr, ni = ab_re - 1.0, ab_im
    q_re = (nr * lre + ni * lim) / den
    q_im = (ni * lre - nr * lim) / den
    b_re, b_im = bre_ref[...], bim_ref[...]
    bb_re = q_re * b_re - q_im * b_im
    bb_im = q_re * b_im + q_im * b_re
    c_re, c_im = cre_ref[...], cim_ref[...]

    rows = lax.broadcasted_iota(jnp.int32, (S5_HALF, LANES), 0) // S5_N
    cols = lax.broadcasted_iota(jnp.int32, (S5_HALF, LANES), 1) // S5_P
    same_group = rows == cols
    same_group_sq = (lax.broadcasted_iota(jnp.int32, (LANES, LANES), 0) // S5_P
                     == lax.broadcasted_iota(jnp.int32, (LANES, LANES), 1) // S5_P)

    def blk(a):
        return jnp.where(same_group, jnp.concatenate([a] * S5_GPT, axis=0), 0.0)

    def dot_exact(a, b):
        return jnp.dot(a, b, precision=lax.Precision.HIGHEST, preferred_element_type=F32)

    p_re, p_im = jnp.ones_like(ab_re), jnp.zeros_like(ab_re)
    toeplitz = []
    for k in range(S5_T + 1):
        if k < S5_T:
            bl_re, bl_im = _cmul(p_re, p_im, bb_re, bb_im)
            tau = S5_T - 1 - k
            wl_ref[tau * LANES:(tau + 1) * LANES, 0:S5_HALF] = blk(bl_re).T.astype(BF16)
            wl_ref[tau * LANES:(tau + 1) * LANES, S5_HALF:S5_STATE] = blk(bl_im).T.astype(BF16)
            kk = dot_exact(bl_re.T, c_re) - dot_exact(bl_im.T, c_im)
            toeplitz.append(jnp.where(same_group_sq, kk, 0.0).astype(BF16))
        if k >= 1:
            cl_re, cl_im = _cmul(p_re, p_im, c_re, c_im)
            t = k - 1
            ws_ref[0:S5_HALF, t * LANES:(t + 1) * LANES] = blk(cl_re).astype(BF16)
            ws_ref[S5_HALF:S5_STATE, t * LANES:(t + 1) * LANES] = (-blk(cl_im)).astype(BF16)
        p_re, p_im = _cmul(p_re, p_im, ab_re, ab_im)

    zero = jnp.zeros((LANES, LANES), BF16)
    for tau in range(S5_T):
        for t in range(S5_T):
            wt_ref[tau * LANES:(tau + 1) * LANES, t * LANES:(t + 1) * LANES] = (
                toeplitz[t - tau] if t >= tau else zero)

    a_re, a_im = _discretize(slre_ref[...], slim_ref[...], sldt_ref[...])
    step = 1
    while step < S5_T:
        a_re, a_im = _cmul(a_re, a_im, a_re, a_im)
        step *= 2
    at_ref[0:1, :] = a_re
    at_ref[1:2, :] = a_im


def _s5_param_layouts(lam_re, lam_im, log_dt, b_re, b_im, c_re, c_im):
    depth, g, n = lam_re.shape
    p = b_re.shape[-1]
    assert (p, n) == (S5_P, S5_N) and S5_T & (S5_T - 1) == 0
    nt = g // S5_GPT

    def b_layout(a):
        return a.reshape(depth, nt, S5_GPT, n, p).transpose(0, 1, 3, 2, 4).reshape(depth, nt, n, LANES)

    def c_layout(a):
        return a.reshape(depth, nt, S5_GPT, p, n).transpose(0, 1, 4, 2, 3).reshape(depth, nt, n, LANES)

    def lam_layout(a):
        a = a.reshape(depth, nt, S5_GPT, n).transpose(0, 1, 3, 2)
        return jnp.broadcast_to(a[..., None], (depth, nt, n, S5_GPT, p)).reshape(depth, nt, n, LANES)

    ldt = jnp.broadcast_to(log_dt.reshape(depth, nt, 1, S5_GPT, 1), (depth, nt, n, S5_GPT, p))
    ldt = ldt.reshape(depth, nt, n, LANES)
    s_lre = lam_re.reshape(depth, nt, 1, S5_HALF)
    s_lim = lam_im.reshape(depth, nt, 1, S5_HALF)
    s_ldt = jnp.broadcast_to(log_dt.reshape(depth, nt, S5_GPT, 1), (depth, nt, S5_GPT, n))
    s_ldt = s_ldt.reshape(depth, nt, 1, S5_HALF)

    return (b_layout(b_re), b_layout(b_im), c_layout(c_re), c_layout(c_im),
            lam_layout(lam_re), lam_layout(lam_im), ldt, s_lre, s_lim, s_ldt)


def _s5_kernel(u_ref, *rest, batch, chunks, pitch):
    param_refs, (d_ref, y_ref, wt_ref, wl_ref, ws_ref, at_ref, loc_ref, st_ref) = rest[:10], rest[10:]
    _s5_build_operators(*param_refs, wt_ref, wl_ref, ws_ref, at_ref)
    x = u_ref[...]
    loc = jnp.dot(x, wl_ref[...], preferred_element_type=F32)
    slabs = S5_STATE // LANES
    half = slabs // 2
    for k in range(slabs):
        for b in range(batch):
            loc_ref[k, b * pitch:b * pitch + chunks, :] = loc[b * chunks:(b + 1) * chunks, k * LANES:(k + 1) * LANES]
    a_re = [at_ref[0:1, k * LANES:(k + 1) * LANES] for k in range(half)]
    a_im = [at_ref[1:2, k * LANES:(k + 1) * LANES] for k in range(half)]

    def body(c, carry):
        rows = pl.ds(c, batch, stride=pitch)
        new = [None] * slabs
        for k in range(half):
            s_re, s_im = carry[k], carry[half + k]
            st_ref[k, rows, :] = s_re
            st_ref[half + k, rows, :] = s_im
            new[k] = a_re[k] * s_re - a_im[k] * s_im + loc_ref[k, rows, :]
            new[half + k] = a_re[k] * s_im + a_im[k] * s_re + loc_ref[half + k, rows, :]
        return tuple(new)

    carry = (jnp.zeros((batch, LANES), F32),) * slabs
    for c in range(chunks):
        carry = body(c, carry)

    pair = 2 * LANES
    y = jnp.concatenate(
        [jnp.dot(x[:, 0:(c + 1) * pair], wt_ref[0:(c + 1) * pair, c * pair:(c + 1) * pair],
                 preferred_element_type=F32) for c in range(S5_ROW // pair)], axis=-1)
    st = jnp.concatenate(
        [jnp.concatenate([st_ref[k, b * pitch:b * pitch + chunks, :] for b in range(batch)], axis=0)
         for k in range(slabs)], axis=-1).astype(BF16)
    y = y + jnp.dot(st, ws_ref[...], preferred_element_type=F32)
    y_ref[...] = (y + d_ref[...] * x.astype(F32)).astype(BF16)


def _s5(u2, params, d_rows, layer, batch, seq):
    nt, rows, _ = u2.shape
    chunks = seq // S5_T
    assert rows == batch * chunks and chunks % SUBLANES == 0
    pitch = chunks + SUBLANES if (chunks // SUBLANES) % 2 == 0 else chunks
    est = (2 * 2 * rows * S5_ROW * 2 + 2 * (S5_ROW * S5_ROW + 2 * S5_ROW * S5_STATE)
           + 2 * batch * pitch * S5_STATE * 4 + 3 * rows * S5_ROW * 4 + rows * S5_STATE * 2
           + 16 * S5_HALF * LANES * 4)
    tile_spec = pl.BlockSpec((None, None, S5_N, LANES), lambda j: (layer, j, 0, 0))
    lane_spec = pl.BlockSpec((None, None, 1, S5_HALF), lambda j: (layer, j, 0, 0))
    return pl.pallas_call(
        functools.partial(_s5_kernel, batch=batch, chunks=chunks, pitch=pitch),
        grid=(nt,),
        in_specs=[pl.BlockSpec((None, rows, S5_ROW), lambda j: (j, 0, 0))] + [tile_spec] * 7 + [lane_spec] * 3
        + [pl.BlockSpec((None, None, 1, S5_ROW), lambda j: (layer, j, 0, 0))],
        out_specs=pl.BlockSpec((None, rows, S5_ROW), lambda j: (j, 0, 0)),
        out_shape=jax.ShapeDtypeStruct((nt, rows, S5_ROW), BF16),
        scratch_shapes=[
            pltpu.VMEM((S5_ROW, S5_ROW), BF16),
            pltpu.VMEM((S5_ROW, S5_STATE), BF16),
            pltpu.VMEM((S5_STATE, S5_ROW), BF16),
            pltpu.VMEM((2, S5_HALF), F32),
            pltpu.VMEM((S5_STATE // LANES, batch * pitch, LANES), F32),
            pltpu.VMEM((S5_STATE // LANES, batch * pitch, LANES), F32),
        ],
        compiler_params=_params(("parallel",), est),
        name="s5_mix",
    )(u2, *params, d_rows)


def _out_kernel(y_ref, z_ref, x_ref, wglu_ref, wout_ref, gs_ref, g_ref, b_ref, *rest, alpha, cast_next):
    if cast_next:
        nin_ref, o_ref, cin_ref, ys_ref = rest
        cin_ref[...] = nin_ref[...].astype(BF16)
    else:
        o_ref, ys_ref = rest
    nt = y_ref.shape[0]
    tm = x_ref.shape[0]
    for j in range(nt):
        for t in range(S5_T):
            ys_ref[j, pl.ds(t, tm // S5_T, stride=S5_T), :] = y_ref[j, :, t * LANES:(t + 1) * LANES].astype(F32)
    y = jnp.concatenate([ys_ref[j] for j in range(nt)], axis=-1)
    y = _gelu_tanh(y)
    y = y * _sigmoid(jnp.dot(y.astype(BF16), wglu_ref[...], preferred_element_type=F32))
    y = _rms_norm(y, gs_ref[...])
    yz = jnp.concatenate([y.astype(BF16), z_ref[...]], axis=-1)
    mix = jnp.dot(yz, wout_ref[...], preferred_element_type=F32)
    o_ref[...] = _layer_norm(alpha * x_ref[...] + mix, g_ref[...], b_ref[...])


def _out(y2, z, x, w_glu, w_out, g_s5, ln_g, ln_b, layer, alpha, next_w_in=None, next_layer=0, tm=512):
    m, d = x.shape
    nt = y2.shape[0]
    d_s5 = nt * LANES
    d_conv = z.shape[1]
    steps = m // tm
    cast_next = next_w_in is not None
    assert tm % (S5_T * 2 * SUBLANES) == 0
    est = (2 * 2 * tm * d * 4 + 2 * tm * (d_s5 + d_conv) * 2
           + (d_s5 * d_s5 + (d_s5 + d_conv) * d) * 2
           + tm * d_s5 * 4 + 3 * tm * d_s5 * 4 + tm * d * 4)
    in_specs = [
        pl.BlockSpec((nt, tm // S5_T, S5_ROW), lambda i: (0, i, 0)),
        pl.BlockSpec((tm, d_conv), lambda i: (i, 0)),
        pl.BlockSpec((tm, d), lambda i: (i, 0)),
        pl.BlockSpec((None, d_s5, d_s5), lambda i: (0, 0, 0), pipeline_mode=pl.Buffered(1)),
        pl.BlockSpec((None, d_s5 + d_conv, d), lambda i: (0, 0, 0), pipeline_mode=pl.Buffered(1)),
        pl.BlockSpec((None, 1, d_s5), lambda i: (layer, 0, 0)),
        pl.BlockSpec((None, 1, d), lambda i: (layer, 0, 0)),
        pl.BlockSpec((None, 1, d), lambda i: (layer, 0, 0)),
    ]
    out_specs = [pl.BlockSpec((tm, d), lambda i: (i, 0))]
    out_shape = [jax.ShapeDtypeStruct((m, d), F32)]
    args = [y2, z, x, w_glu, w_out, g_s5, ln_g, ln_b]
    if cast_next:
        rows, cols = next_w_in.shape[1] // steps, next_w_in.shape[2]
        assert next_w_in.shape[1] % (steps * 2 * SUBLANES) == 0
        in_specs.append(pl.BlockSpec((None, rows, cols), lambda i: (next_layer, i, 0)))
        out_specs.append(pl.BlockSpec((None, rows, cols), lambda i: (0, i, 0)))
        out_shape.append(jax.ShapeDtypeStruct((1,) + next_w_in.shape[1:], BF16))
        args.append(next_w_in)
        est += 2 * rows * cols * 6
    outs = pl.pallas_call(
        functools.partial(_out_kernel, alpha=alpha, cast_next=cast_next),
        grid=(steps,),
        in_specs=in_specs,
        out_specs=out_specs,
        out_shape=out_shape,
        scratch_shapes=[pltpu.VMEM((nt, tm, LANES), F32)],
        compiler_params=_params(("arbitrary",), est),
        name="glu_out_cast" if cast_next else "glu_out",
    )(*args)
    return outs[0], (outs[1] if cast_next else None)


def kernel(x, ffn1_gate, ffn1_up, ffn1_down, ln1_g, ln1_b, w_in, s5_lam_re, s5_lam_im, s5_log_dt,
           s5_b_re, s5_b_im, s5_c_re, s5_c_im, s5_d, s5_w_glu, conv_w, conv_b, g_s5, g_conv, w_out,
           ln2_g, ln2_b, ffn2_gate, ffn2_up, ffn2_down, ln3_g, ln3_b):
    batch, seq, d_model = x.shape
    depth = ffn1_gate.shape[0]
    d_s5 = s5_w_glu.shape[-1]
    alpha = (2.0 * depth) ** 0.25
    assert d_s5 % LANES == 0 and seq % S5_T == 0

    def row(a):
        return a.reshape(depth, 1, a.shape[-1])

    ffns = []
    for l in range(depth):
        ffns.append(((ffn1_gate, ffn1_up, ffn1_down), row(ln1_g), row(ln1_b), l))
        ffns.append(((ffn2_gate, ffn2_up, ffn2_down), row(ln3_g), row(ln3_b), l))
    w_in_b = w_in[0:1].astype(BF16)

    s5_params = _s5_param_layouts(s5_lam_re, s5_lam_im, s5_log_dt, s5_b_re, s5_b_im, s5_c_re, s5_c_im)
    nt = d_s5 // LANES
    d_rows = jnp.tile(s5_d.reshape(depth, nt, 1, LANES), (1, 1, 1, S5_T))

    state = {"h": x.reshape(batch * seq, d_model), "weights": ffns[0][0], "wlayer": 0, "k": 0}

    def ffn():
        k = state["k"]
        _, g, b, l = ffns[k]
        nxt = ffns[k + 1] if k + 1 < len(ffns) else None
        state["h"], cast = _ffn(state["h"], state["weights"], state["wlayer"], g, b, l, alpha,
                                nxt[0] if nxt else None, nxt[3] if nxt else 0)
        state["weights"], state["wlayer"], state["k"] = cast, 0, k + 1

    for l in range(depth):
        ffn()
        h = state["h"]
        u2, z, w_glu_b, w_out_b = _proj(h, w_in_b, conv_w, row(conv_b), row(g_conv), s5_w_glu, w_out,
                                        l, seq, d_s5)
        y2 = _s5(u2, s5_params, d_rows, l, batch, seq)
        state["h"], w_in_b = _out(y2, z, h, w_glu_b, w_out_b, row(g_s5), row(ln2_g), row(ln2_b), l, alpha,
                                  w_in if l + 1 < depth else None, l + 1)
        ffn()
    return state["h"].reshape(batch, seq, d_model)
```

```python
import functools

import jax
import jax.numpy as jnp
from jax import lax
from jax.experimental import pallas as pl
from jax.experimental.pallas import tpu as pltpu

F32 = jnp.float32
BF16 = jnp.bfloat16

LN_EPS = 1e-5
RMS_EPS = 1e-6
CONV_W = 3
FFN_BETA = 0.5

LANES = 128
SUBLANES = 8
MXU_COLS = 256
V7X_VMEM_BYTES = 64 * 1024 * 1024
VMEM_CAP_BYTES = V7X_VMEM_BYTES - 2 * 1024 * 1024
COMPILER_TEMP_BYTES = 8 * 1024 * 1024

S5_P = 16
S5_N = 64
S5_T = 8
S5_GPT = LANES // S5_P
S5_HALF = S5_GPT * S5_N
S5_STATE = 2 * S5_HALF
S5_ROW = S5_T * LANES


def _layer_norm(y, g, b, eps=LN_EPS):
    mu = jnp.mean(y, axis=-1, keepdims=True)
    yc = y - mu
    var = jnp.mean(yc * yc, axis=-1, keepdims=True)
    return yc * lax.rsqrt(var + eps) * g + b


def _rms_norm(y, g):
    return y * lax.rsqrt(jnp.mean(y * y, axis=-1, keepdims=True) + RMS_EPS) * g


def _sigmoid(x):
    return 1.0 / (1.0 + jnp.exp(-x))


def _gelu_tanh(x):
    c = 0.7978845608028654
    return 0.5 * x * (1.0 + jnp.tanh(c * (x + 0.044715 * (x * x * x))))


def _cmul(ar, ai, br, bi):
    return ar * br - ai * bi, ar * bi + ai * br


def _params(sem, est_bytes):
    limit = min(VMEM_CAP_BYTES, est_bytes + COMPILER_TEMP_BYTES)
    return pltpu.CompilerParams(dimension_semantics=sem, vmem_limit_bytes=limit)


def _ffn_kernel(x_ref, wg_ref, wu_ref, wd_ref, g_ref, b_ref, *rest, alpha, cast_next):
    if cast_next:
        ng_ref, nu_ref, nd_ref, o_ref, cg_ref, cu_ref, cd_ref, xb_ref = rest
    else:
        o_ref, xb_ref = rest
    f = pl.program_id(1)
    last = pl.num_programs(1) - 1

    def weights():
        if cast_next:
            cg_ref[...] = ng_ref[...].astype(BF16)
            cu_ref[...] = nu_ref[...].astype(BF16)
            cd_ref[...] = nd_ref[...].astype(BF16)
        return wg_ref[...].astype(BF16), wu_ref[...].astype(BF16), wd_ref[...].astype(BF16)

    def swiglu_down(xb, wg, wu, wd):
        hs = []
        for c0 in range(0, wg.shape[1], MXU_COLS):
            gate = jnp.dot(xb, wg[:, c0:c0 + MXU_COLS], preferred_element_type=F32)
            up = jnp.dot(xb, wu[:, c0:c0 + MXU_COLS], preferred_element_type=F32)
            hs.append(((gate * _sigmoid(gate)) * up).astype(BF16))
        h = hs[0] if len(hs) == 1 else jnp.concatenate(hs, axis=-1)
        return jnp.dot(h, wd, preferred_element_type=F32)

    @pl.when(f == 0)
    def _():
        x = x_ref[...]
        xb = x.astype(BF16)
        xb_ref[...] = xb
        o_ref[...] = (alpha / FFN_BETA) * x + swiglu_down(xb, *weights())

    @pl.when(jnp.logical_and(f > 0, f < last))
    def _():
        o_ref[...] += swiglu_down(xb_ref[...], *weights())

    @pl.when(f == last)
    def _():
        acc = o_ref[...] + swiglu_down(xb_ref[...], *weights())
        o_ref[...] = _layer_norm(acc, g_ref[...], b_ref[...], LN_EPS / (FFN_BETA * FFN_BETA))


def _ffn(x, weights, wlayer, ln_g, ln_b, layer, alpha, next_weights=None, next_layer=0, tm=1024):
    wg, wu, wd = weights
    m, d = x.shape
    ff = wg.shape[-1]
    wbytes = wg.dtype.itemsize
    tf = MXU_COLS * (4 // wbytes)
    ni, nf = m // tm, ff // tf
    cast_next = next_weights is not None
    dr = d // ni
    assert ff % tf == 0 and nf >= 2 and m % tm == 0 and d % (ni * LANES) == 0
    est = (2 * 2 * tm * d * 4 + 2 * 3 * d * tf * wbytes + tm * d * 2 + (wbytes // 4) * 3 * d * tf * 2
           + 2 * tm * tf * 4 + tm * tf * 2 + cast_next * 2 * 3 * dr * tf * 6)
    in_specs = [
        pl.BlockSpec((tm, d), lambda i, f: (i, 0)),
        pl.BlockSpec((None, d, tf), lambda i, f: (wlayer, 0, f)),
        pl.BlockSpec((None, d, tf), lambda i, f: (wlayer, 0, f)),
        pl.BlockSpec((None, tf, d), lambda i, f: (wlayer, f, 0)),
        pl.BlockSpec((None, 1, d), lambda i, f: (layer, 0, 0)),
        pl.BlockSpec((None, 1, d), lambda i, f: (layer, 0, 0)),
    ]
    out_specs = [pl.BlockSpec((tm, d), lambda i, f: (i, 0))]
    out_shape = [jax.ShapeDtypeStruct((m, d), F32)]
    args = [x, wg, wu, wd, ln_g, ln_b]
    if cast_next:
        in_specs += [
            pl.BlockSpec((None, dr, tf), lambda i, f: (next_layer, i, f)),
            pl.BlockSpec((None, dr, tf), lambda i, f: (next_layer, i, f)),
            pl.BlockSpec((None, tf, dr), lambda i, f: (next_layer, f, i)),
        ]
        out_specs += [
            pl.BlockSpec((None, dr, tf), lambda i, f: (0, i, f)),
            pl.BlockSpec((None, dr, tf), lambda i, f: (0, i, f)),
            pl.BlockSpec((None, tf, dr), lambda i, f: (0, f, i)),
        ]
        out_shape += [jax.ShapeDtypeStruct((1,) + w.shape[1:], BF16) for w in next_weights]
        args += list(next_weights)
    outs = pl.pallas_call(
        functools.partial(_ffn_kernel, alpha=alpha, cast_next=cast_next),
        grid=(ni, nf),
        in_specs=in_specs,
        out_specs=out_specs,
        out_shape=out_shape,
        scratch_shapes=[pltpu.VMEM((tm, d), BF16)],
        compiler_params=_params(("arbitrary", "arbitrary"), est),
        name="ffn_cast" if cast_next else "ffn",
    )(*args)
    return outs[0], tuple(outs[1:])


def _proj_kernel(x_ref, win_ref, cw_ref, cb_ref, gc_ref, nglu_ref, nout_ref,
                 u_ref, z_ref, cglu_ref, cout_ref, us_ref, vpad_ref, *, tiles_per_seq):
    tm = x_ref.shape[0]
    nt = u_ref.shape[0]
    d_s5 = nt * LANES
    d_conv = z_ref.shape[1]

    @pl.when(pl.program_id(0) % tiles_per_seq == 0)
    def _():
        vpad_ref[0:SUBLANES, :] = jnp.zeros((SUBLANES, d_conv), F32)

    cglu_ref[...] = nglu_ref[...].astype(BF16)
    cout_ref[...] = nout_ref[...].astype(BF16)

    xb = x_ref[...].astype(BF16)

    u = jnp.dot(xb, win_ref[:, 0:d_s5], preferred_element_type=F32)
    for j in range(nt):
        us_ref[j] = u[:, j * LANES:(j + 1) * LANES]
    for j in range(nt):
        for t in range(S5_T):
            rows = us_ref[j, pl.ds(t, tm // S5_T, stride=S5_T), :]
            u_ref[j, :, t * LANES:(t + 1) * LANES] = rows.astype(BF16)

    o = d_s5
    gate_c = jnp.dot(xb, win_ref[:, o + d_conv:o + 2 * d_conv], preferred_element_type=F32)
    hid = jnp.dot(xb, win_ref[:, o + 2 * d_conv:o + 3 * d_conv], preferred_element_type=F32)
    v = gate_c * hid
    vpad_ref[SUBLANES:SUBLANES + tm, :] = v
    v1 = vpad_ref[SUBLANES - 1:SUBLANES - 1 + tm, :]
    v2 = vpad_ref[SUBLANES - 2:SUBLANES - 2 + tm, :]
    conv = cb_ref[...] + cw_ref[0:1, :] * v2 + cw_ref[1:2, :] * v1 + cw_ref[2:3, :] * v
    vpad_ref[0:SUBLANES, :] = vpad_ref[tm:tm + SUBLANES, :]

    gate_b = jnp.dot(xb, win_ref[:, o:o + d_conv], preferred_element_type=F32)
    z_ref[...] = _rms_norm(gate_b * conv, gc_ref[...]).astype(BF16)


def _proj(x, w_in, conv_w, conv_b, g_conv, w_glu, w_out, layer, seq, d_s5, tm=512):
    m, d = x.shape
    d_in = w_in.shape[-1]
    d_conv = conv_b.shape[-1]
    nt = d_s5 // LANES
    steps = m // tm
    rg, ro = w_glu.shape[1] // steps, w_out.shape[1] // steps
    assert CONV_W - 1 <= SUBLANES and conv_w.shape[1] == CONV_W and seq % tm == 0
    assert tm % (S5_T * 2 * SUBLANES) == 0
    assert w_glu.shape[1] % (steps * 2 * SUBLANES) == 0 and w_out.shape[1] % (steps * 2 * SUBLANES) == 0
    est = (2 * tm * d * 4 + d * d_in * 2 + 2 * tm * (d_s5 + d_conv) * 2
           + tm * d_s5 * 4 + (tm + SUBLANES) * d_conv * 4 + tm * d * 2 + 6 * tm * d_conv * 4
           + 2 * (rg * w_glu.shape[2] + ro * w_out.shape[2]) * 6)
    return pl.pallas_call(
        functools.partial(_proj_kernel, tiles_per_seq=seq // tm),
        grid=(steps,),
        in_specs=[
            pl.BlockSpec((tm, d), lambda i: (i, 0)),
            pl.BlockSpec((None, d, d_in), lambda i: (0, 0, 0), pipeline_mode=pl.Buffered(1)),
            pl.BlockSpec((None, CONV_W, d_conv), lambda i: (layer, 0, 0)),
            pl.BlockSpec((None, 1, d_conv), lambda i: (layer, 0, 0)),
            pl.BlockSpec((None, 1, d_conv), lambda i: (layer, 0, 0)),
            pl.BlockSpec((None, rg, w_glu.shape[2]), lambda i: (layer, i, 0)),
            pl.BlockSpec((None, ro, w_out.shape[2]), lambda i: (layer, i, 0)),
        ],
        out_specs=[
            pl.BlockSpec((nt, tm // S5_T, S5_ROW), lambda i: (0, i, 0)),
            pl.BlockSpec((tm, d_conv), lambda i: (i, 0)),
            pl.BlockSpec((None, rg, w_glu.shape[2]), lambda i: (0, i, 0)),
            pl.BlockSpec((None, ro, w_out.shape[2]), lambda i: (0, i, 0)),
        ],
        out_shape=[
            jax.ShapeDtypeStruct((nt, m // S5_T, S5_ROW), BF16),
            jax.ShapeDtypeStruct((m, d_conv), BF16),
            jax.ShapeDtypeStruct((1,) + w_glu.shape[1:], BF16),
            jax.ShapeDtypeStruct((1,) + w_out.shape[1:], BF16),
        ],
        scratch_shapes=[pltpu.VMEM((nt, tm, LANES), F32), pltpu.VMEM((tm + SUBLANES, d_conv), F32)],
        compiler_params=_params(("arbitrary",), est),
        name="proj_conv",
    )(x, w_in, conv_w, conv_b, g_conv, w_glu, w_out)


def _discretize(lre, lim, ldt):
    dt = jnp.exp(ldt)
    mag = jnp.exp(lre * dt)
    ang = lim * dt
    return mag * jnp.cos(ang), mag * jnp.sin(ang)


def _s5_build_operators(bre_ref, bim_ref, cre_ref, cim_ref, lre_ref, lim_ref, ldt_ref,
                        slre_ref, slim_ref, sldt_ref, wt_ref, wl_ref, ws_ref, at_ref):
    lre, lim = lre_ref[...], lim_ref[...]
    ab_re, ab_im = _discretize(lre, lim, ldt_ref[...])
    den = lre * lre + lim * lim
    nr, ni = ab_re - 1.0, ab_im
    q_re = (nr * lre + ni * lim) / den
    q_im = (ni * lre - nr * lim) / den
    b_re, b_im = bre_ref[...], bim_ref[...]
    bb_re = q_re * b_re - q_im * b_im
    bb_im = q_re * b_im + q_im * b_re
    c_re, c_im = cre_ref[...], cim_ref[...]

    rows = lax.broadcasted_iota(jnp.int32, (S5_HALF, LANES), 0) // S5_N
    cols = lax.broadcasted_iota(jnp.int32, (S5_HALF, LANES), 1) // S5_P
    same_group = rows == cols
    same_group_sq = (lax.broadcasted_iota(jnp.int32, (LANES, LANES), 0) // S5_P
                     == lax.broadcasted_iota(jnp.int32, (LANES, LANES), 1) // S5_P)

    def blk(a):
        return jnp.where(same_group, jnp.concatenate([a] * S5_GPT, axis=0), 0.0)

    def dot_exact(a, b):
        return jnp.dot(a, b, precision=lax.Precision.HIGHEST, preferred_element_type=F32)

    p_re, p_im = jnp.ones_like(ab_re), jnp.zeros_like(ab_re)
    toeplitz = []
    for k in range(S5_T + 1):
        if k < S5_T:
            bl_re, bl_im = _cmul(p_re, p_im, bb_re, bb_im)
            tau = S5_T - 1 - k
            wl_ref[tau * LANES:(tau + 1) * LANES, 0:S5_HALF] = blk(bl_re).T.astype(BF16)
            wl_ref[tau * LANES:(tau + 1) * LANES, S5_HALF:S5_STATE] = blk(bl_im).T.astype(BF16)
            kk = dot_exact(bl_re.T, c_re) - dot_exact(bl_im.T, c_im)
            toeplitz.append(jnp.where(same_group_sq, kk, 0.0).astype(BF16))
        if k >= 1:
            cl_re, cl_im = _cmul(p_re, p_im, c_re, c_im)
            t = k - 1
            ws_ref[0:S5_HALF, t * LANES:(t + 1) * LANES] = blk(cl_re).astype(BF16)
            ws_ref[S5_HALF:S5_STATE, t * LANES:(t + 1) * LANES] = (-blk(cl_im)).astype(BF16)
        p_re, p_im = _cmul(p_re, p_im, ab_re, ab_im)

    zero = jnp.zeros((LANES, LANES), BF16)
    for tau in range(S5_T):
        for t in range(S5_T):
            wt_ref[tau * LANES:(tau + 1) * LANES, t * LANES:(t + 1) * LANES] = (
                toeplitz[t - tau] if t >= tau else zero)

    a_re, a_im = _discretize(slre_ref[...], slim_ref[...], sldt_ref[...])
    step = 1
    while step < S5_T:
        a_re, a_im = _cmul(a_re, a_im, a_re, a_im)
        step *= 2
    at_ref[0:1, :] = a_re
    at_ref[1:2, :] = a_im


def _s5_param_layouts(lam_re, lam_im, log_dt, b_re, b_im, c_re, c_im):
    depth, g, n = lam_re.shape
    p = b_re.shape[-1]
    assert (p, n) == (S5_P, S5_N) and S5_T & (S5_T - 1) == 0
    nt = g // S5_GPT

    def b_layout(a):
        return a.reshape(depth, nt, S5_GPT, n, p).transpose(0, 1, 3, 2, 4).reshape(depth, nt, n, LANES)

    def c_layout(a):
        return a.reshape(depth, nt, S5_GPT, p, n).transpose(0, 1, 4, 2, 3).reshape(depth, nt, n, LANES)

    def lam_layout(a):
        a = a.reshape(depth, nt, S5_GPT, n).transpose(0, 1, 3, 2)
        return jnp.broadcast_to(a[..., None], (depth, nt, n, S5_GPT, p)).reshape(depth, nt, n, LANES)

    ldt = jnp.broadcast_to(log_dt.reshape(depth, nt, 1, S5_GPT, 1), (depth, nt, n, S5_GPT, p))
    ldt = ldt.reshape(depth, nt, n, LANES)
    s_lre = lam_re.reshape(depth, nt, 1, S5_HALF)
    s_lim = lam_im.reshape(depth, nt, 1, S5_HALF)
    s_ldt = jnp.broadcast_to(log_dt.reshape(depth, nt, S5_GPT, 1), (depth, nt, S5_GPT, n))
    s_ldt = s_ldt.reshape(depth, nt, 1, S5_HALF)

    return (b_layout(b_re), b_layout(b_im), c_layout(c_re), c_layout(c_im),
            lam_layout(lam_re), lam_layout(lam_im), ldt, s_lre, s_lim, s_ldt)


def _s5_kernel(u_ref, *rest, batch, chunks, pitch):
    param_refs, (d_ref, y_ref, wt_ref, wl_ref, ws_ref, at_ref, loc_ref, st_ref) = rest[:10], rest[10:]
    _s5_build_operators(*param_refs, wt_ref, wl_ref, ws_ref, at_ref)
    x = u_ref[...]
    loc = jnp.dot(x, wl_ref[...], preferred_element_type=F32)
    slabs = S5_STATE // LANES
    half = slabs // 2
    for k in range(slabs):
        for b in range(batch):
            loc_ref[k, b * pitch:b * pitch + chunks, :] = loc[b * chunks:(b + 1) * chunks, k * LANES:(k + 1) * LANES]
    a_re = [at_ref[0:1, k * LANES:(k + 1) * LANES] for k in range(half)]
    a_im = [at_ref[1:2, k * LANES:(k + 1) * LANES] for k in range(half)]

    def body(c, carry):
        rows = pl.ds(c, batch, stride=pitch)
        new = [None] * slabs
        for k in range(half):
            s_re, s_im = carry[k], carry[half + k]
            st_ref[k, rows, :] = s_re
            st_ref[half + k, rows, :] = s_im
            new[k] = a_re[k] * s_re - a_im[k] * s_im + loc_ref[k, rows, :]
            new[half + k] = a_re[k] * s_im + a_im[k] * s_re + loc_ref[half + k, rows, :]
        return tuple(new)

    carry = (jnp.zeros((batch, LANES), F32),) * slabs
    for c in range(chunks):
        carry = body(c, carry)

    pair = 2 * LANES
    y = jnp.concatenate(
        [jnp.dot(x[:, 0:(c + 1) * pair], wt_ref[0:(c + 1) * pair, c * pair:(c + 1) * pair],
                 preferred_element_type=F32) for c in range(S5_ROW // pair)], axis=-1)
    st = jnp.concatenate(
        [jnp.concatenate([st_ref[k, b * pitch:b * pitch + chunks, :] for b in range(batch)], axis=0)
         for k in range(slabs)], axis=-1).astype(BF16)
    y = y + jnp.dot(st, ws_ref[...], preferred_element_type=F32)
    y_ref[...] = (y + d_ref[...] * x.astype(F32)).astype(BF16)


def _s5(u2, params, d_rows, layer, batch, seq):
    nt, rows, _ = u2.shape
    chunks = seq // S5_T
    assert rows == batch * chunks and chunks % SUBLANES == 0
    pitch = chunks + SUBLANES if (chunks // SUBLANES) % 2 == 0 else chunks
    est = (2 * 2 * rows * S5_ROW * 2 + 2 * (S5_ROW * S5_ROW + 2 * S5_ROW * S5_STATE)
           + 2 * batch * pitch * S5_STATE * 4 + 3 * rows * S5_ROW * 4 + rows * S5_STATE * 2
           + 16 * S5_HALF * LANES * 4)
    tile_spec = pl.BlockSpec((None, None, S5_N, LANES), lambda j: (layer, j, 0, 0))
    lane_spec = pl.BlockSpec((None, None, 1, S5_HALF), lambda j: (layer, j, 0, 0))
    return pl.pallas_call(
        functools.partial(_s5_kernel, batch=batch, chunks=chunks, pitch=pitch),
        grid=(nt,),
        in_specs=[pl.BlockSpec((None, rows, S5_ROW), lambda j: (j, 0, 0))] + [tile_spec] * 7 + [lane_spec] * 3
        + [pl.BlockSpec((None, None, 1, S5_ROW), lambda j: (layer, j, 0, 0))],
        out_specs=pl.BlockSpec((None, rows, S5_ROW), lambda j: (j, 0, 0)),
        out_shape=jax.ShapeDtypeStruct((nt, rows, S5_ROW), BF16),
        scratch_shapes=[
            pltpu.VMEM((S5_ROW, S5_ROW), BF16),
            pltpu.VMEM((S5_ROW, S5_STATE), BF16),
            pltpu.VMEM((S5_STATE, S5_ROW), BF16),
            pltpu.VMEM((2, S5_HALF), F32),
            pltpu.VMEM((S5_STATE // LANES, batch * pitch, LANES), F32),
            pltpu.VMEM((S5_STATE // LANES, batch * pitch, LANES), F32),
        ],
        compiler_params=_params(("parallel",), est),
        name="s5_mix",
    )(u2, *params, d_rows)


def _out_kernel(y_ref, z_ref, x_ref, wglu_ref, wout_ref, gs_ref, g_ref, b_ref, *rest, alpha, cast_next):
    if cast_next:
        nin_ref, o_ref, cin_ref, ys_ref = rest
        cin_ref[...] = nin_ref[...].astype(BF16)
    else:
        o_ref, ys_ref = rest
    nt = y_ref.shape[0]
    tm = x_ref.shape[0]
    for j in range(nt):
        for t in range(S5_T):
            ys_ref[j, pl.ds(t, tm // S5_T, stride=S5_T), :] = y_ref[j, :, t * LANES:(t + 1) * LANES].astype(F32)
    y = jnp.concatenate([ys_ref[j] for j in range(nt)], axis=-1)
    y = _gelu_tanh(y)
    y = y * _sigmoid(jnp.dot(y.astype(BF16), wglu_ref[...], preferred_element_type=F32))
    y = _rms_norm(y, gs_ref[...])
    yz = jnp.concatenate([y.astype(BF16), z_ref[...]], axis=-1)
    mix = jnp.dot(yz, wout_ref[...], preferred_element_type=F32)
    o_ref[...] = _layer_norm(alpha * x_ref[...] + mix, g_ref[...], b_ref[...])


def _out(y2, z, x, w_glu, w_out, g_s5, ln_g, ln_b, layer, alpha, next_w_in=None, next_layer=0, tm=512):
    m, d = x.shape
    nt = y2.shape[0]
    d_s5 = nt * LANES
    d_conv = z.shape[1]
    steps = m // tm
    cast_next = next_w_in is not None
    assert tm % (S5_T * 2 * SUBLANES) == 0
    est = (2 * 2 * tm * d * 4 + 2 * tm * (d_s5 + d_conv) * 2
           + (d_s5 * d_s5 + (d_s5 + d_conv) * d) * 2
           + tm * d_s5 * 4 + 3 * tm * d_s5 * 4 + tm * d * 4)
    in_specs = [
        pl.BlockSpec((nt, tm // S5_T, S5_ROW), lambda i: (0, i, 0)),
        pl.BlockSpec((tm, d_conv), lambda i: (i, 0)),
        pl.BlockSpec((tm, d), lambda i: (i, 0)),
        pl.BlockSpec((None, d_s5, d_s5), lambda i: (0, 0, 0), pipeline_mode=pl.Buffered(1)),
        pl.BlockSpec((None, d_s5 + d_conv, d), lambda i: (0, 0, 0), pipeline_mode=pl.Buffered(1)),
        pl.BlockSpec((None, 1, d_s5), lambda i: (layer, 0, 0)),
        pl.BlockSpec((None, 1, d), lambda i: (layer, 0, 0)),
        pl.BlockSpec((None, 1, d), lambda i: (layer, 0, 0)),
    ]
    out_specs = [pl.BlockSpec((tm, d), lambda i: (i, 0))]
    out_shape = [jax.ShapeDtypeStruct((m, d), F32)]
    args = [y2, z, x, w_glu, w_out, g_s5, ln_g, ln_b]
    if cast_next:
        rows, cols = next_w_in.shape[1] // steps, next_w_in.shape[2]
        assert next_w_in.shape[1] % (steps * 2 * SUBLANES) == 0
        in_specs.append(pl.BlockSpec((None, rows, cols), lambda i: (next_layer, i, 0)))
        out_specs.append(pl.BlockSpec((None, rows, cols), lambda i: (0, i, 0)))
        out_shape.append(jax.ShapeDtypeStruct((1,) + next_w_in.shape[1:], BF16))
        args.append(next_w_in)
        est += 2 * rows * cols * 6
    outs = pl.pallas_call(
        functools.partial(_out_kernel, alpha=alpha, cast_next=cast_next),
        grid=(steps,),
        in_specs=in_specs,
        out_specs=out_specs,
        out_shape=out_shape,
        scratch_shapes=[pltpu.VMEM((nt, tm, LANES), F32)],
        compiler_params=_params(("arbitrary",), est),
        name="glu_out_cast" if cast_next else "glu_out",
    )(*args)
    return outs[0], (outs[1] if cast_next else None)


def kernel(x, ffn1_gate, ffn1_up, ffn1_down, ln1_g, ln1_b, w_in, s5_lam_re, s5_lam_im, s5_log_dt,
           s5_b_re, s5_b_im, s5_c_re, s5_c_im, s5_d, s5_w_glu, conv_w, conv_b, g_s5, g_conv, w_out,
           ln2_g, ln2_b, ffn2_gate, ffn2_up, ffn2_down, ln3_g, ln3_b):
    batch, seq, d_model = x.shape
    depth = ffn1_gate.shape[0]
    d_s5 = s5_w_glu.shape[-1]
    alpha = (2.0 * depth) ** 0.25
    assert d_s5 % LANES == 0 and seq % S5_T == 0

    def row(a):
        return a.reshape(depth, 1, a.shape[-1])

    ffns = []
    for l in range(depth):
        ffns.append(((ffn1_gate, ffn1_up, ffn1_down), row(ln1_g), row(ln1_b), l))
        ffns.append(((ffn2_gate, ffn2_up, ffn2_down), row(ln3_g), row(ln3_b), l))
    w_in_b = w_in[0:1].astype(BF16)

    s5_params = _s5_param_layouts(s5_lam_re, s5_lam_im, s5_log_dt, s5_b_re, s5_b_im, s5_c_re, s5_c_im)
    nt = d_s5 // LANES
    d_rows = jnp.tile(s5_d.reshape(depth, nt, 1, LANES), (1, 1, 1, S5_T))

    state = {"h": x.reshape(batch * seq, d_model), "weights": ffns[0][0], "wlayer": 0, "k": 0}

    def ffn():
        k = state["k"]
        _, g, b, l = ffns[k]
        nxt = ffns[k + 1] if k + 1 < len(ffns) else None
        state["h"], cast = _ffn(state["h"], state["weights"], state["wlayer"], g, b, l, alpha,
                                nxt[0] if nxt else None, nxt[3] if nxt else 0)
        state["weights"], state["wlayer"], state["k"] = cast, 0, k + 1

    for l in range(depth):
        ffn()
        h = state["h"]
        u2, z, w_glu_b, w_out_b = _proj(h, w_in_b, conv_w, row(conv_b), row(g_conv), s5_w_glu, w_out,
                                        l, seq, d_s5)
        y2 = _s5(u2, s5_params, d_rows, l, batch, seq)
        state["h"], w_in_b = _out(y2, z, h, w_glu_b, w_out_b, row(g_s5), row(ln2_g), row(ln2_b), l, alpha,
                                  w_in if l + 1 < depth else None, l + 1)
        ffn()
    return state["h"].reshape(batch, seq, d_model)
```
